```python
import jax, jax.numpy as jnp
from jax import lax
import numpy as np


D_MODEL = 1024
BATCH = 8
SEQ = 4096
DEPTH = 2

DN_HEADS = 4
DN_DK = 128
DN_DV = 128
DN_QK = DN_HEADS * DN_DK
DN_V = DN_HEADS * DN_DV
DN_CONV = 5
CHUNK = 64
N_DIR = 2
CF_WIDTH = 512
CF_CONV = 31
N_BRANCH = 2
EPS = 1e-6

SIZES = (2 * DN_QK + DN_V,
         DN_V,
         N_DIR * DN_HEADS,
         N_DIR * DN_HEADS,
         CF_WIDTH,
         CF_WIDTH,
         CF_WIDTH,
         N_BRANCH * D_MODEL)
D_IN = sum(SIZES)

kernel_name = 'hybrid_gated_deltanet_conformer_encoder'


def rms_norm(x, w):
    xf = x.astype(jnp.float32)
    y = xf * lax.rsqrt(jnp.mean(xf * xf, axis=-1, keepdims=True) + EPS)
    return (y * w.astype(jnp.float32)).astype(x.dtype)


def layer_norm(x, w, b):
    xf = x.astype(jnp.float32)
    mu = jnp.mean(xf, axis=-1, keepdims=True)
    xc = xf - mu
    y = xc * lax.rsqrt(jnp.mean(xc * xc, axis=-1, keepdims=True) + EPS)
    return (y * w.astype(jnp.float32) + b.astype(jnp.float32)).astype(x.dtype)


def l2_normalize(t):
    return t * lax.rsqrt(jnp.sum(t * t, axis=-1, keepdims=True) + EPS)


def depthwise_conv_centred(u, w):
    k = w.shape[0]
    return lax.conv_general_dilated(
        u, w[:, None, :].astype(u.dtype), window_strides=(1,),
        padding=[(k // 2, k // 2)], dimension_numbers=('NWC', 'WIO', 'NWC'),
        feature_group_count=u.shape[-1])


def gated_delta_chunked(q, k, v, g, beta):
    bsz, seq, nh, dk = q.shape
    dv = v.shape[-1]
    n = seq // CHUNK
    q = q.reshape(bsz, n, CHUNK, nh, dk).transpose(0, 3, 1, 2, 4) * (dk ** -0.5)
    k = k.reshape(bsz, n, CHUNK, nh, dk).transpose(0, 3, 1, 2, 4)
    v = v.reshape(bsz, n, CHUNK, nh, dv).transpose(0, 3, 1, 2, 4)
    g = jnp.cumsum(g.reshape(bsz, n, CHUNK, nh).transpose(0, 3, 1, 2), axis=-1)
    beta = beta.reshape(bsz, n, CHUNK, nh).transpose(0, 3, 1, 2)

    incl = jnp.tril(jnp.ones((CHUNK, CHUNK), dtype=bool))
    strict = jnp.tril(jnp.ones((CHUNK, CHUNK), dtype=bool), k=-1)
    diff = g[..., :, None] - g[..., None, :]
    decay = jnp.where(incl, jnp.exp(jnp.where(incl, diff, 0.0)), 0.0)

    k_beta = k * beta[..., None]
    lmat = jnp.where(strict, jnp.einsum('bhncd,bhnsd->bhncs', k_beta, k) * decay, 0.0)
    eye = jnp.eye(CHUNK, dtype=jnp.float32)
    rhs = jnp.concatenate([v * beta[..., None], k_beta * jnp.exp(g)[..., None]], axis=-1)
    sol = lax.linalg.triangular_solve(eye + lmat, rhs, left_side=True, lower=True,
                                      unit_diagonal=True)
    value, k_cum = sol[..., :dv], sol[..., dv:]

    attn = jnp.einsum('bhncd,bhnsd->bhncs', q, k) * decay
    q_dec = q * jnp.exp(g)[..., None]
    g_last = g[..., -1]
    k_dec = k * jnp.exp(g_last[..., None] - g)[..., None]

    xs = (jnp.moveaxis(value, 2, 0), jnp.moveaxis(k_cum, 2, 0), jnp.moveaxis(attn, 2, 0),
          jnp.moveaxis(q_dec, 2, 0), jnp.moveaxis(k_dec, 2, 0),
          jnp.moveaxis(jnp.exp(g_last), 2, 0))

    def step(state, inp):
        val_n, kc_n, a_n, qd_n, kd_n, dl_n = inp
        v_new = val_n - jnp.einsum('bhcd,bhde->bhce', kc_n, state)
        o_n = jnp.einsum('bhcd,bhde->bhce', qd_n, state) + jnp.einsum('bhcs,bhse->bhce', a_n, v_new)
        state = state * dl_n[..., None, None] + jnp.einsum('bhcd,bhce->bhde', kd_n, v_new)
        return state, o_n

    s0 = jnp.zeros((bsz, nh, dk, dv), dtype=jnp.float32)
    _, o = lax.scan(step, s0, xs)
    return o.transpose(1, 0, 3, 2, 4).reshape(bsz, seq, nh, dv)


def hybrid_mixer(h, w_in, qkv_conv_w, a_log, dt_bias, dn_norm_w, w_dn_out,
                 cf_conv_w, cf_conv_b, cf_ln_w, cf_ln_b, w_cf_out, gate_b, w_out):
    bsz, seq, _ = h.shape
    pts = []
    acc = 0
    for s in SIZES[:-1]:
        acc += s
        pts.append(acc)
    proj = h @ w_in
    qkv, z_a, a_lg, b_lg, cf_val, cf_glu, z_b, gate_lg = jnp.split(proj, pts, axis=-1)

    qkv = jax.nn.silu(depthwise_conv_centred(qkv, qkv_conv_w))
    q = qkv[..., :DN_QK].astype(jnp.float32).reshape(bsz, seq, DN_HEADS, DN_DK)
    k = qkv[..., DN_QK:2 * DN_QK].astype(jnp.float32).reshape(bsz, seq, DN_HEADS, DN_DK)
    v = qkv[..., 2 * DN_QK:].astype(jnp.float32).reshape(bsz, seq, DN_HEADS, DN_DV)
    q = l2_normalize(q)
    k = l2_normalize(k)
    a_lg = a_lg.astype(jnp.float32).reshape(bsz, seq, N_DIR, DN_HEADS)
    g = -jnp.exp(a_log.astype(jnp.float32)) * jax.nn.softplus(a_lg + dt_bias.astype(jnp.float32))
    beta = jax.nn.sigmoid(b_lg.astype(jnp.float32).reshape(bsz, seq, N_DIR, DN_HEADS))
    o_fwd = gated_delta_chunked(q, k, v, g[:, :, 0], beta[:, :, 0])
    flip = lambda t: jnp.flip(t, axis=1)
    o_bwd = flip(gated_delta_chunked(flip(q), flip(k), flip(v), flip(g[:, :, 1]), flip(beta[:, :, 1])))
    o = o_fwd + o_bwd
    o = o * lax.rsqrt(jnp.mean(o * o, axis=-1, keepdims=True) + EPS) * dn_norm_w.astype(jnp.float32)
    o = o * jax.nn.silu(z_a.astype(jnp.float32).reshape(bsz, seq, DN_HEADS, DN_DV))
    y_a = o.reshape(bsz, seq, DN_V).astype(h.dtype) @ w_dn_out

    u = cf_val * jax.nn.sigmoid(cf_glu)
    u = depthwise_conv_centred(u, cf_conv_w) + cf_conv_b
    u = jax.nn.silu(layer_norm(u, cf_ln_w, cf_ln_b))
    y_b = (u * jax.nn.silu(z_b)) @ w_cf_out

    gate = jax.nn.sigmoid(gate_lg + gate_b).reshape(bsz, seq, N_BRANCH, D_MODEL)
    y = gate[:, :, 0] * y_a + gate[:, :, 1] * y_b
    return y @ w_out


def setup_inputs(seed: int = 0) -> dict:
    key = jax.random.key(seed)
    ks = jax.random.split(key, 17)
    nrm = jax.random.normal
    x = nrm(ks[0], (BATCH, SEQ, D_MODEL), jnp.float32)
    norm_w = 1.0 + 0.05 * nrm(ks[1], (DEPTH, D_MODEL), jnp.float32)
    w_in = nrm(ks[2], (DEPTH, D_MODEL, D_IN), jnp.float32) * D_MODEL ** -0.5
    qkv_conv_w = nrm(ks[3], (DEPTH, DN_CONV, 2 * DN_QK + DN_V), jnp.float32) * DN_CONV ** -0.5
    a_log = jnp.log(jax.random.uniform(ks[4], (DEPTH, N_DIR, DN_HEADS), jnp.float32, 1.0, 16.0))
    dt = jnp.exp(jax.random.uniform(ks[5], (DEPTH, N_DIR, DN_HEADS), jnp.float32,
                                    float(np.log(1e-3)), float(np.log(1e-1))))
    dt_bias = dt + jnp.log(-jnp.expm1(-dt))
    dn_norm_w = 1.0 + 0.05 * nrm(ks[6], (DEPTH, DN_DV), jnp.float32)
    w_dn_out = nrm(ks[7], (DEPTH, DN_V, D_MODEL), jnp.float32) * DN_V ** -0.5
    cf_conv_w = nrm(ks[8], (DEPTH, CF_CONV, CF_WIDTH), jnp.float32) * CF_CONV ** -0.5
    cf_conv_b = 0.02 * nrm(ks[9], (DEPTH, CF_WIDTH), jnp.float32)
    cf_ln_w = 1.0 + 0.05 * nrm(ks[10], (DEPTH, CF_WIDTH), jnp.float32)
    cf_ln_b = 0.02 * nrm(ks[11], (DEPTH, CF_WIDTH), jnp.float32)
    w_cf_out = nrm(ks[12], (DEPTH, CF_WIDTH, D_MODEL), jnp.float32) * CF_WIDTH ** -0.5
    gate_b = 0.02 * nrm(ks[13], (DEPTH, N_BRANCH * D_MODEL), jnp.float32)
    w_out = nrm(ks[14], (DEPTH, D_MODEL, D_MODEL), jnp.float32) * D_MODEL ** -0.5
    final_norm_w = 1.0 + 0.05 * nrm(ks[15], (D_MODEL,), jnp.float32)
    return {'x': x, 'norm_w': norm_w, 'w_in': w_in, 'qkv_conv_w': qkv_conv_w,
            'a_log': a_log, 'dt_bias': dt_bias, 'dn_norm_w': dn_norm_w, 'w_dn_out': w_dn_out,
            'cf_conv_w': cf_conv_w, 'cf_conv_b': cf_conv_b, 'cf_ln_w': cf_ln_w, 'cf_ln_b': cf_ln_b,
            'w_cf_out': w_cf_out, 'gate_b': gate_b, 'w_out': w_out, 'final_norm_w': final_norm_w}


def reference(x, norm_w, w_in, qkv_conv_w, a_log, dt_bias, dn_norm_w, w_dn_out,
              cf_conv_w, cf_conv_b, cf_ln_w, cf_ln_b, w_cf_out, gate_b, w_out, final_norm_w):
    for l in range(DEPTH):
        h = rms_norm(x, norm_w[l])
        x = x + hybrid_mixer(h, w_in[l], qkv_conv_w[l], a_log[l], dt_bias[l], dn_norm_w[l],
                             w_dn_out[l], cf_conv_w[l], cf_conv_b[l], cf_ln_w[l], cf_ln_b[l],
                             w_cf_out[l], gate_b[l], w_out[l])
    return rms_norm(x, final_norm_w)
```

```python
import functools

import jax
import jax.numpy as jnp
from jax import lax
from jax.experimental import pallas as pl
from jax.experimental.pallas import tpu as pltpu

EPS = 1e-6
D_MODEL = 1024
N_HEADS = 4
HEAD_DIM = 128
DN_QK = N_HEADS * HEAD_DIM
DN_V = N_HEADS * HEAD_DIM
DN_CONV = 5
CF_WIDTH = 512
CF_CONV = 31
N_DIR = 2
N_CHAIN = N_DIR * N_HEADS
QKV = 2 * DN_QK + DN_V

LANES = 128
V7X_VMEM_BYTES = 64 * 1024 * 1024
VMEM_LIMIT_BYTES = V7X_VMEM_BYTES - 6 * 1024 * 1024

TOKEN_TILE = 512
HALO = 16
CHUNK = 128
ROW_BLOCK = 64

BF16 = jnp.bfloat16
F32 = jnp.float32


def _sigmoid(x):
    return 1.0 / (1.0 + jnp.exp(-x))


def _silu(x):
    return x * _sigmoid(x)


def _softplus(x):
    return jnp.maximum(x, 0.0) + jnp.log1p(jnp.exp(-jnp.abs(x)))


def _mm(a, b):
    return jnp.dot(a.astype(BF16), b.astype(BF16), preferred_element_type=F32)


def _mm_nt(a, b):
    return lax.dot_general(a.astype(BF16), b.astype(BF16), (((1,), (1,)), ((), ())),
                           preferred_element_type=F32)


def _mm_exact(a, b):
    return jnp.dot(a, b, preferred_element_type=F32, precision=lax.Precision.HIGHEST)


def _proj_kernel(x_ref, xp_ref, xn_ref, nw_ref, wconv_ref, wrest_ref, wabt_ref, qkvw_ref,
                 cfw_ref, cfb_ref, lnw_ref, lnb_ref, wcf_ref, gateb_ref, arow_ref, dtrow_ref,
                 acol_ref, dtcol_ref,
                 q_ref, k_ref, v_ref, za_ref, g0_ref, g1yb_ref, gbtok_ref, gblane_ref,
                 hall_ref, pc_ref, u_ref, ub_ref, zb_ref):
    tm = x_ref.shape[0]
    i = pl.program_id(1)
    n_tiles = pl.num_programs(1)
    nw = nw_ref[...]

    def rms(xv):
        return xv * lax.rsqrt(jnp.mean(xv * xv, axis=-1, keepdims=True) + EPS) * nw

    hall_ref[0:HALO, :] = jnp.where(i > 0, rms(xp_ref[...]), 0.0).astype(BF16)
    hall_ref[HALO:HALO + tm, :] = rms(x_ref[...]).astype(BF16)
    hall_ref[HALO + tm:, :] = jnp.where(i < n_tiles - 1, rms(xn_ref[...]), 0.0).astype(BF16)

    pc_ref[...] = jnp.dot(hall_ref[...], wconv_ref[...], preferred_element_type=F32)
    h_main = hall_ref[HALO:HALO + tm, :]

    za_ref[...] = _silu(jnp.dot(h_main, wrest_ref[:, 0:DN_V], preferred_element_type=F32))
    zb_ref[...] = jnp.dot(h_main, wrest_ref[:, DN_V:DN_V + CF_WIDTH], preferred_element_type=F32)

    ab_off = DN_V + CF_WIDTH + 2 * D_MODEL
    ab_tok = jnp.dot(h_main, wrest_ref[:, ab_off:ab_off + LANES],
                     preferred_element_type=F32)[:, 0:2 * N_CHAIN]
    col_id = lax.broadcasted_iota(jnp.int32, ab_tok.shape, 1)
    g_tok = -jnp.exp(arow_ref[...]) * _softplus(ab_tok + dtrow_ref[...])
    gbtok_ref[...] = jnp.where(col_id < N_CHAIN, g_tok, _sigmoid(ab_tok))
    ab_lane = lax.dot_general(wabt_ref[...], h_main, (((1,), (1,)), ((), ())),
                              preferred_element_type=F32)
    row_id = lax.broadcasted_iota(jnp.int32, ab_lane.shape, 0)
    g_lane = -jnp.exp(acol_ref[...]) * _softplus(ab_lane + dtcol_ref[...])
    gb_lane = jnp.where(row_id < N_CHAIN, g_lane, _sigmoid(ab_lane))
    for c in range(tm // CHUNK):
        gblane_ref[c] = gb_lane[:, c * CHUNK:(c + 1) * CHUNK]

    off5 = HALO - DN_CONV // 2
    for rb in range(tm // ROW_BLOCK):
        r0 = rb * ROW_BLOCK
        for cg in range(QKV // LANES):
            c0 = cg * LANES
            acc = None
            for j in range(DN_CONV):
                t = pc_ref[r0 + off5 + j:r0 + off5 + j + ROW_BLOCK, c0:c0 + LANES] * qkvw_ref[j:j + 1, c0:c0 + LANES]
                acc = t if acc is None else acc + t
            s = _silu(acc)
            hc = (cg % N_HEADS) * HEAD_DIM
            if cg < 2 * N_HEADS:
                s = s * lax.rsqrt(jnp.sum(s * s, axis=-1, keepdims=True) + EPS)
            if cg < N_HEADS:
                q_ref[r0:r0 + ROW_BLOCK, hc:hc + HEAD_DIM] = (s * (HEAD_DIM ** -0.5)).astype(BF16)
            elif cg < 2 * N_HEADS:
                k_ref[r0:r0 + ROW_BLOCK, hc:hc + HEAD_DIM] = s.astype(BF16)
            else:
                v_ref[r0:r0 + ROW_BLOCK, hc:hc + HEAD_DIM] = s.astype(BF16)

    u_ref[...] = pc_ref[:, QKV:QKV + CF_WIDTH] * _sigmoid(pc_ref[:, QKV + CF_WIDTH:QKV + 2 * CF_WIDTH])
    off31 = HALO - CF_CONV // 2
    for rb in range(tm // ROW_BLOCK):
        r0 = rb * ROW_BLOCK
        pieces = []
        for cg in range(CF_WIDTH // LANES):
            c0 = cg * LANES
            acc = None
            for j in range(CF_CONV):
                t = u_ref[r0 + off31 + j:r0 + off31 + j + ROW_BLOCK, c0:c0 + LANES] * cfw_ref[j:j + 1, c0:c0 + LANES]
                acc = t if acc is None else acc + t
            pieces.append(acc)
        cv = jnp.concatenate(pieces, axis=1) + cfb_ref[...]
        mu = jnp.mean(cv, axis=-1, keepdims=True)
        xc = cv - mu
        y = xc * lax.rsqrt(jnp.mean(xc * xc, axis=-1, keepdims=True) + EPS) * lnw_ref[...] + lnb_ref[...]
        ub_ref[r0:r0 + ROW_BLOCK, :] = (_silu(y) * _silu(zb_ref[r0:r0 + ROW_BLOCK, :])).astype(BF16)
    y_b = jnp.dot(ub_ref[...], wcf_ref[...], preferred_element_type=F32)

    g_off = DN_V + CF_WIDTH
    g0_ref[...] = _sigmoid(jnp.dot(h_main, wrest_ref[:, g_off:g_off + D_MODEL], preferred_element_type=F32)
                           + gateb_ref[:, 0:D_MODEL])
    gate1 = _sigmoid(jnp.dot(h_main, wrest_ref[:, g_off + D_MODEL:g_off + 2 * D_MODEL],
                             preferred_element_type=F32) + gateb_ref[:, D_MODEL:2 * D_MODEL])
    g1yb_ref[...] = gate1 * y_b


def _tri_inverse(lm, eye):
    x = -lm
    p = eye + x
    n_steps = CHUNK.bit_length() - 2
    for _ in range(n_steps):
        xb = x.astype(BF16)
        x = jnp.dot(xb, xb, preferred_element_type=F32)
        p = p + _mm(p, x)
    return p


def _delta_kernel(qf_ref, kf_ref, vf_ref, qb_ref, kb_ref, vb_ref, gtf_ref, gtb_ref, glf_ref, glb_ref,
                  of_ref, ob_ref, state_ref):
    tm = qf_ref.shape[0]
    n_chunks = tm // CHUNK
    i = pl.program_id(1)

    @pl.when(i == 0)
    def _():
        state_ref[...] = jnp.zeros_like(state_ref)

    rid = lax.broadcasted_iota(jnp.int32, (CHUNK, CHUNK), 0)
    cid = lax.broadcasted_iota(jnp.int32, (CHUNK, CHUNK), 1)
    lower = rid >= cid
    upper = rid <= cid
    lower_f = lower.astype(F32)
    upper_f = upper.astype(F32)
    eye = (rid == cid).astype(F32)

    def chunk_step(n, carry):
        rows = (pl.multiple_of(n * CHUNK, CHUNK), pl.multiple_of((n_chunks - 1 - n) * CHUNK, CHUNK))
        cidx = (n, n_chunks - 1 - n)
        gt_refs = (gtf_ref, gtb_ref)
        gl_refs = (glf_ref, glb_ref)
        q_refs = (qf_ref, qb_ref)
        k_refs = (kf_ref, kb_ref)
        v_refs = (vf_ref, vb_ref)
        o_refs = (of_ref, ob_ref)
        for d in range(N_DIR):
            r0 = rows[d]
            gb_tok = gt_refs[d][pl.ds(r0, CHUNK), :]
            gb_lane = gl_refs[d][cidx[d]]
            if d == 0:
                gcum_tok = _mm_exact(lower_f, gb_tok)
                gcum_lane = _mm_exact(gb_lane, upper_f)
                mask, strict = lower, rid > cid
            else:
                gcum_tok = _mm_exact(upper_f, gb_tok)
                gcum_lane = _mm_exact(gb_lane, lower_f)
                mask, strict = upper, rid < cid
            for h in range(N_HEADS):
                c = d * N_HEADS + h
                hs = slice(h * HEAD_DIM, (h + 1) * HEAD_DIM)
                q = q_refs[d][pl.ds(r0, CHUNK), hs]
                k = k_refs[d][pl.ds(r0, CHUNK), hs]
                v = v_refs[d][pl.ds(r0, CHUNK), hs].astype(F32)
                kf = k.astype(F32)
                gcol = gcum_tok[:, c:c + 1]
                grow = gcum_lane[c:c + 1, :]
                beta = gb_tok[:, N_CHAIN + c:N_CHAIN + c + 1]
                gtot = gcum_lane[c:c + 1, CHUNK - 1:CHUNK] if d == 0 else gcum_lane[c:c + 1, 0:1]
                decay = jnp.where(mask, jnp.exp(jnp.where(mask, gcol - grow, 0.0)), 0.0)
                k_beta = kf * beta
                e_col = jnp.exp(gcol)
                kk = _mm_nt(jnp.concatenate([k_beta.astype(BF16), q], axis=0), k)
                lm = jnp.where(strict, kk[0:CHUNK] * decay, 0.0)
                attn = kk[CHUNK:2 * CHUNK] * decay
                t_inv = _tri_inverse(lm, eye)
                rhs = jnp.concatenate([v * beta, k_beta * e_col], axis=1)
                sol = _mm(t_inv, rhs)
                value = sol[:, 0:HEAD_DIM]
                k_cum = sol[:, HEAD_DIM:2 * HEAD_DIM]
                q_dec = q.astype(F32) * e_col
                k_dec = kf * jnp.exp(gtot - gcol)
                state = state_ref[c]
                r = _mm(jnp.concatenate([k_cum.astype(BF16), q_dec.astype(BF16)], axis=0), state)
                v_new = (value - r[0:CHUNK]).astype(BF16)
                o = r[CHUNK:2 * CHUNK] + _mm(attn, v_new)
                o_refs[d][pl.ds(r0, CHUNK), hs] = o
                state_ref[c] = state * jnp.exp(gtot) + _mm(k_dec.T, v_new)
        return carry

    lax.fori_loop(0, n_chunks, chunk_step, 0)


def _out_kernel(of_ref, ob_ref, za_ref, g0_ref, g1yb_ref, x_ref, dnw_ref, wdn_ref, wout_ref, fnw_ref,
                y_ref, *, final_norm):
    o = of_ref[...] + ob_ref[...]
    heads = []
    for h in range(N_HEADS):
        oh = o[:, h * HEAD_DIM:(h + 1) * HEAD_DIM]
        heads.append(oh * lax.rsqrt(jnp.mean(oh * oh, axis=-1, keepdims=True) + EPS) * dnw_ref[...])
    on = jnp.concatenate(heads, axis=1) * za_ref[...]
    y_a = _mm(on, wdn_ref[...])
    y = g0_ref[...] * y_a + g1yb_ref[...]
    out = x_ref[...] + _mm(y, wout_ref[...])
    if final_norm:
        out = out * lax.rsqrt(jnp.mean(out * out, axis=-1, keepdims=True) + EPS) * fnw_ref[...]
    y_ref[...] = out


def _const_spec(shape):
    nd = len(shape)
    return pl.BlockSpec(shape, lambda b, i: (0,) * nd, pipeline_mode=pl.Buffered(1))


def _compiler_params():
    return pltpu.CompilerParams(dimension_semantics=("arbitrary", "arbitrary"),
                                vmem_limit_bytes=VMEM_LIMIT_BYTES)


def _proj_call(x, nw, wconv, wrest, wabt, qkvw, cfw, cfb, lnw, lnb, wcf, gateb, arow, dtrow, acol, dtcol):
    bsz, seq, _ = x.shape
    tm = TOKEN_TILE
    n_tiles = seq // tm
    hb = tm // HALO
    n_hblocks = seq // HALO

    def tile3(width, dtype):
        return jax.ShapeDtypeStruct((bsz, seq, width), dtype), pl.BlockSpec((None, tm, width), lambda b, i: (b, i, 0))

    outs = [tile3(DN_QK, BF16), tile3(DN_QK, BF16), tile3(DN_V, BF16), tile3(DN_V, F32),
            tile3(D_MODEL, F32), tile3(D_MODEL, F32), tile3(2 * N_CHAIN, F32)]
    out_shapes = [o[0] for o in outs] + [jax.ShapeDtypeStruct((bsz, seq // CHUNK, 2 * N_CHAIN, CHUNK), F32)]
    out_specs = [o[1] for o in outs] + [pl.BlockSpec((None, tm // CHUNK, 2 * N_CHAIN, CHUNK),
                                                     lambda b, i: (b, i, 0, 0))]
    in_specs = [
        pl.BlockSpec((None, tm, D_MODEL), lambda b, i: (b, i, 0)),
        pl.BlockSpec((None, HALO, D_MODEL), lambda b, i: (b, jnp.maximum(i * hb - 1, 0), 0)),
        pl.BlockSpec((None, HALO, D_MODEL), lambda b, i: (b, jnp.minimum((i + 1) * hb, n_hblocks - 1), 0)),
    ] + [_const_spec(a.shape) for a in (nw, wconv, wrest, wabt, qkvw, cfw, cfb, lnw, lnb, wcf, gateb,
                                        arow, dtrow, acol, dtcol)]
    scratch = [
        pltpu.VMEM((tm + 2 * HALO, D_MODEL), BF16),
        pltpu.VMEM((tm + 2 * HALO, QKV + 2 * CF_WIDTH), F32),
        pltpu.VMEM((tm + 2 * HALO, CF_WIDTH), F32),
        pltpu.VMEM((tm, CF_WIDTH), BF16),
        pltpu.VMEM((tm, CF_WIDTH), F32),
    ]
    return pl.pallas_call(
        _proj_kernel, grid=(bsz, n_tiles), in_specs=in_specs, out_specs=out_specs, out_shape=out_shapes,
        scratch_shapes=scratch, compiler_params=_compiler_params(), name="proj_prep",
    )(x, x, x, nw, wconv, wrest, wabt, qkvw, cfw, cfb, lnw, lnb, wcf, gateb, arow, dtrow, acol, dtcol)


def _delta_call(q, k, v, gbtok, gblane):
    bsz, seq, _ = q.shape
    tm = TOKEN_TILE
    n_tiles = seq // tm
    fwd3 = lambda w: pl.BlockSpec((None, tm, w), lambda b, i: (b, i, 0))
    bwd3 = lambda w: pl.BlockSpec((None, tm, w), lambda b, i: (b, n_tiles - 1 - i, 0))
    lane_f = pl.BlockSpec((None, tm // CHUNK, 2 * N_CHAIN, CHUNK), lambda b, i: (b, i, 0, 0))
    lane_b = pl.BlockSpec((None, tm // CHUNK, 2 * N_CHAIN, CHUNK), lambda b, i: (b, n_tiles - 1 - i, 0, 0))
    in_specs = [fwd3(DN_QK), fwd3(DN_QK), fwd3(DN_V), bwd3(DN_QK), bwd3(DN_QK), bwd3(DN_V),
                fwd3(2 * N_CHAIN), bwd3(2 * N_CHAIN), lane_f, lane_b]
    out_shape = [jax.ShapeDtypeStruct((bsz, seq, DN_V), F32)] * 2
    return pl.pallas_call(
        _delta_kernel, grid=(bsz, n_tiles), in_specs=in_specs, out_specs=[fwd3(DN_V), bwd3(DN_V)],
        out_shape=out_shape, scratch_shapes=[pltpu.VMEM((N_CHAIN, HEAD_DIM, HEAD_DIM), F32)],
        compiler_params=_compiler_params(), name="delta_scan",
    )(q, k, v, q, k, v, gbtok, gbtok, gblane, gblane)


def _out_call(o_f, o_b, za, g0, g1yb, x, dnw, wdn, wout, fnw, final_norm):
    bsz, seq, _ = x.shape
    tm = TOKEN_TILE
    tile = lambda w: pl.BlockSpec((None, tm, w), lambda b, i: (b, i, 0))
    in_specs = [tile(DN_V), tile(DN_V), tile(DN_V), tile(D_MODEL), tile(D_MODEL), tile(D_MODEL)] + \
               [_const_spec(a.shape) for a in (dnw, wdn, wout, fnw)]
    return pl.pallas_call(
        functools.partial(_out_kernel, final_norm=final_norm), grid=(bsz, seq // tm), in_specs=in_specs,
        out_specs=tile(D_MODEL), out_shape=jax.ShapeDtypeStruct(x.shape, F32),
        compiler_params=_compiler_params(), name="merge_out",
    )(o_f, o_b, za, g0, g1yb, x, dnw, wdn, wout, fnw)


def kernel(x, norm_w, w_in, qkv_conv_w, a_log, dt_bias, dn_norm_w, w_dn_out, cf_conv_w, cf_conv_b,
           cf_ln_w, cf_ln_b, w_cf_out, gate_b, w_out, final_norm_w):
    depth = w_in.shape[0]
    bsz, seq, d_model = x.shape
    assert d_model == D_MODEL and seq % TOKEN_TILE == 0 and TOKEN_TILE % CHUNK == 0

    o_za = QKV
    o_ab = o_za + DN_V
    o_cf = o_ab + 2 * N_CHAIN
    o_zb = o_cf + 2 * CF_WIDTH
    o_gate = o_zb + CF_WIDTH

    for l in range(depth):
        w = w_in[l]
        wconv = jnp.concatenate([w[:, 0:QKV], w[:, o_cf:o_cf + 2 * CF_WIDTH]], axis=1).astype(BF16)
        w_ab = w[:, o_ab:o_ab + 2 * N_CHAIN]
        wrest = jnp.concatenate([w[:, o_za:o_za + DN_V], w[:, o_zb:o_zb + CF_WIDTH],
                                 w[:, o_gate:o_gate + 2 * D_MODEL],
                                 jnp.pad(w_ab, ((0, 0), (0, LANES - 2 * N_CHAIN)))], axis=1).astype(BF16)
        wabt = w_ab.T.astype(BF16)
        pad8 = lambda t: jnp.concatenate([t.reshape(-1), jnp.zeros((N_CHAIN,), F32)])
        a16, dt16 = pad8(a_log[l]), pad8(dt_bias[l])
        q, k, v, za, g0, g1yb, gbtok, gblane = _proj_call(
            x, norm_w[l][None, :], wconv, wrest, wabt, qkv_conv_w[l], cf_conv_w[l], cf_conv_b[l][None, :],
            cf_ln_w[l][None, :], cf_ln_b[l][None, :], w_cf_out[l].astype(BF16), gate_b[l][None, :],
            a16[None, :], dt16[None, :], a16[:, None], dt16[:, None])
        o_f, o_b = _delta_call(q, k, v, gbtok, gblane)
        x = _out_call(o_f, o_b, za, g0, g1yb, x, dn_norm_w[l][None, :], w_dn_out[l].astype(BF16),
                      w_out[l].astype(BF16), final_norm_w[None, :], final_norm=(l == depth - 1))
    return x
```

```python
import functools

import jax
import jax.numpy as jnp
from jax import lax
from jax.experimental import pallas as pl
from jax.experimental.pallas import tpu as pltpu

EPS = 1e-6
D_MODEL = 1024
N_HEADS = 4
HEAD_DIM = 128
DN_QK = N_HEADS * HEAD_DIM
DN_V = N_HEADS * HEAD_DIM
DN_CONV = 5
CF_WIDTH = 512
CF_CONV = 31
N_DIR = 2
N_CHAIN = N_DIR * N_HEADS
QKV = 2 * DN_QK + DN_V

LANES = 128
SUBLANES = 8
V7X_VMEM_BYTES = 64 * 1024 * 1024
VMEM_LIMIT_BYTES = V7X_VMEM_BYTES - 6 * 1024 * 1024

TOKEN_TILE = 512
HALO = 16
CHUNK = 128
ROW_BLOCK = 64

BF16 = jnp.bfloat16
F32 = jnp.float32


def _sigmoid(x):
    return 1.0 / (1.0 + jnp.exp(-x))


def _silu(x):
    return x * _sigmoid(x)


def _softplus(x):
    return jnp.maximum(x, 0.0) + jnp.log1p(jnp.exp(-jnp.abs(x)))


def _mm(a, b):
    return jnp.dot(a.astype(BF16), b.astype(BF16), preferred_element_type=F32)


def _mm_nt(a, b):
    return lax.dot_general(a.astype(BF16), b.astype(BF16), (((1,), (1,)), ((), ())),
                           preferred_element_type=F32)


def _mm_exact(a, b):
    return jnp.dot(a, b, preferred_element_type=F32, precision=lax.Precision.HIGHEST)


def _proj_kernel(x_ref, xp_ref, xn_ref, nw_ref, wconv_ref, wrest_ref, wabt_ref, qkvw_ref,
                 cfw_ref, cfb_ref, lnw_ref, lnb_ref, wcf_ref, gateb_ref, arow_ref, dtrow_ref,
                 acol_ref, dtcol_ref,
                 q_ref, k_ref, v_ref, za_ref, g0_ref, g1yb_ref, gbtok_ref, gblane_ref,
                 hall_ref, pc_ref, u_ref, ush_ref, ub_ref, zb_ref):
    tm = x_ref.shape[0]
    i = pl.program_id(1)
    n_tiles = pl.num_programs(1)
    nw = nw_ref[...]

    def rms(xv):
        return xv * lax.rsqrt(jnp.mean(xv * xv, axis=-1, keepdims=True) + EPS) * nw

    hall_ref[0:HALO, :] = jnp.where(i > 0, rms(xp_ref[...]), 0.0).astype(BF16)
    hall_ref[HALO:HALO + tm, :] = rms(x_ref[...]).astype(BF16)
    hall_ref[HALO + tm:, :] = jnp.where(i < n_tiles - 1, rms(xn_ref[...]), 0.0).astype(BF16)

    pc_ref[...] = jnp.dot(hall_ref[...], wconv_ref[...], preferred_element_type=F32)
    h_main = hall_ref[HALO:HALO + tm, :]

    za_ref[...] = _silu(jnp.dot(h_main, wrest_ref[:, 0:DN_V], preferred_element_type=F32))
    zb_ref[...] = jnp.dot(h_main, wrest_ref[:, DN_V:DN_V + CF_WIDTH], preferred_element_type=F32)

    ab_off = DN_V + CF_WIDTH + 2 * D_MODEL
    ab_tok = jnp.dot(h_main, wrest_ref[:, ab_off:ab_off + LANES],
                     preferred_element_type=F32)[:, 0:2 * N_CHAIN]
    col_id = lax.broadcasted_iota(jnp.int32, ab_tok.shape, 1)
    g_tok = -jnp.exp(arow_ref[...]) * _softplus(ab_tok + dtrow_ref[...])
    gbtok_ref[...] = jnp.where(col_id < N_CHAIN, g_tok, _sigmoid(ab_tok))
    ab_lane = lax.dot_general(wabt_ref[...], h_main, (((1,), (1,)), ((), ())),
                              preferred_element_type=F32)
    row_id = lax.broadcasted_iota(jnp.int32, ab_lane.shape, 0)
    g_lane = -jnp.exp(acol_ref[...]) * _softplus(ab_lane + dtcol_ref[...])
    gb_lane = jnp.where(row_id < N_CHAIN, g_lane, _sigmoid(ab_lane))
    for c in range(tm // CHUNK):
        gblane_ref[c] = gb_lane[:, c * CHUNK:(c + 1) * CHUNK]

    off5 = HALO - DN_CONV // 2
    for rb in range(tm // ROW_BLOCK):
        r0 = rb * ROW_BLOCK
        for cg in range(QKV // LANES):
            c0 = cg * LANES
            acc = None
            for j in range(DN_CONV):
                t = pc_ref[r0 + off5 + j:r0 + off5 + j + ROW_BLOCK, c0:c0 + LANES] * qkvw_ref[j:j + 1, c0:c0 + LANES]
                acc = t if acc is None else acc + t
            s = _silu(acc)
            hc = (cg % N_HEADS) * HEAD_DIM
            if cg < 2 * N_HEADS:
                s = s * lax.rsqrt(jnp.sum(s * s, axis=-1, keepdims=True) + EPS)
            if cg < N_HEADS:
                q_ref[r0:r0 + ROW_BLOCK, hc:hc + HEAD_DIM] = (s * (HEAD_DIM ** -0.5)).astype(BF16)
            elif cg < 2 * N_HEADS:
                k_ref[r0:r0 + ROW_BLOCK, hc:hc + HEAD_DIM] = s.astype(BF16)
            else:
                v_ref[r0:r0 + ROW_BLOCK, hc:hc + HEAD_DIM] = s.astype(BF16)

    u_ref[...] = pc_ref[:, QKV:QKV + CF_WIDTH] * _sigmoid(pc_ref[:, QKV + CF_WIDTH:QKV + 2 * CF_WIDTH])
    sh_rows = ush_ref.shape[1]
    for s in range(1, SUBLANES):
        ush_ref[s - 1] = u_ref[s:s + sh_rows, :]
    off31 = HALO - CF_CONV // 2
    for rb in range(tm // ROW_BLOCK):
        r0 = rb * ROW_BLOCK
        pieces = []
        for cg in range(CF_WIDTH // LANES):
            c0 = cg * LANES
            acc = None
            for j in range(CF_CONV):
                s = (off31 + j) % SUBLANES
                a0 = r0 + off31 + j - s
                if s == 0:
                    tap = u_ref[a0:a0 + ROW_BLOCK, c0:c0 + LANES]
                else:
                    tap = ush_ref[s - 1, a0:a0 + ROW_BLOCK, c0:c0 + LANES]
                t = tap * cfw_ref[j:j + 1, c0:c0 + LANES]
                acc = t if acc is None else acc + t
            pieces.append(acc)
        cv = jnp.concatenate(pieces, axis=1) + cfb_ref[...]
        mu = jnp.mean(cv, axis=-1, keepdims=True)
        xc = cv - mu
        y = xc * lax.rsqrt(jnp.mean(xc * xc, axis=-1, keepdims=True) + EPS) * lnw_ref[...] + lnb_ref[...]
        ub_ref[r0:r0 + ROW_BLOCK, :] = (_silu(y) * _silu(zb_ref[r0:r0 + ROW_BLOCK, :])).astype(BF16)
    y_b = jnp.dot(ub_ref[...], wcf_ref[...], preferred_element_type=F32)

    g_off = DN_V + CF_WIDTH
    g0_ref[...] = _sigmoid(jnp.dot(h_main, wrest_ref[:, g_off:g_off + D_MODEL], preferred_element_type=F32)
                           + gateb_ref[:, 0:D_MODEL])
    gate1 = _sigmoid(jnp.dot(h_main, wrest_ref[:, g_off + D_MODEL:g_off + 2 * D_MODEL],
                             preferred_element_type=F32) + gateb_ref[:, D_MODEL:2 * D_MODEL])
    g1yb_ref[...] = gate1 * y_b


def _delta_kernel(qf_ref, kf_ref, vf_ref, qb_ref, kb_ref, vb_ref, gtf_ref, gtb_ref, glf_ref, glb_ref,
                  of_ref, ob_ref, state_ref):
    tm = qf_ref.shape[0]
    n_chunks = tm // CHUNK
    i = pl.program_id(1)

    @pl.when(i == 0)
    def _():
        state_ref[...] = jnp.zeros_like(state_ref)

    rid = lax.broadcasted_iota(jnp.int32, (CHUNK, CHUNK), 0)
    cid = lax.broadcasted_iota(jnp.int32, (CHUNK, CHUNK), 1)
    lower = rid >= cid
    upper = rid <= cid
    lower_f = lower.astype(F32)
    upper_f = upper.astype(F32)
    eye = (rid == cid).astype(F32)
    chains = range(N_CHAIN)
    same_block = [(rid >> j) == (cid >> j) for j in range(1, CHUNK.bit_length())]
    level_masks = [same_block[0]] + [same_block[j] & jnp.logical_not(same_block[j - 1])
                                     for j in range(1, len(same_block))]

    def chunk_step(n, carry):
        rows = (pl.multiple_of(n * CHUNK, CHUNK), pl.multiple_of((n_chunks - 1 - n) * CHUNK, CHUNK))
        cidx = (n, n_chunks - 1 - n)
        gt_refs = (gtf_ref, gtb_ref)
        gl_refs = (glf_ref, glb_ref)
        q_refs = (qf_ref, qb_ref)
        k_refs = (kf_ref, kb_ref)
        v_refs = (vf_ref, vb_ref)
        o_refs = (of_ref, ob_ref)

        k, decay, strict, rhs, q_dec, k_dec, e_tot, lhs_kk = [], [], [], [], [], [], [], []
        for d in range(N_DIR):
            r0 = rows[d]
            gb_tok = gt_refs[d][pl.ds(r0, CHUNK), :]
            gb_lane = gl_refs[d][cidx[d]]
            if d == 0:
                gcum_tok = _mm_exact(lower_f, gb_tok)
                gcum_lane = _mm_exact(gb_lane, upper_f)
                mask, strict_d = lower, rid > cid
            else:
                gcum_tok = _mm_exact(upper_f, gb_tok)
                gcum_lane = _mm_exact(gb_lane, lower_f)
                mask, strict_d = upper, rid < cid
            for h in range(N_HEADS):
                c = d * N_HEADS + h
                hs = slice(h * HEAD_DIM, (h + 1) * HEAD_DIM)
                q_c = q_refs[d][pl.ds(r0, CHUNK), hs]
                k_c = k_refs[d][pl.ds(r0, CHUNK), hs]
                v_c = v_refs[d][pl.ds(r0, CHUNK), hs].astype(F32)
                kf_c = k_c.astype(F32)
                gcol = gcum_tok[:, c:c + 1]
                grow = gcum_lane[c:c + 1, :]
                beta = gb_tok[:, N_CHAIN + c:N_CHAIN + c + 1]
                gtot = gcum_lane[c:c + 1, CHUNK - 1:CHUNK] if d == 0 else gcum_lane[c:c + 1, 0:1]
                k_beta = kf_c * beta
                e_col = jnp.exp(gcol)
                k.append(k_c)
                strict.append(strict_d)
                decay.append(jnp.where(mask, jnp.exp(jnp.where(mask, gcol - grow, 0.0)), 0.0))
                lhs_kk.append(jnp.concatenate([k_beta.astype(BF16), q_c], axis=0))
                rhs.append(jnp.concatenate([v_c * beta, k_beta * e_col], axis=1).astype(BF16))
                q_dec.append((q_c.astype(F32) * e_col).astype(BF16))
                k_dec.append(kf_c * jnp.exp(gtot - gcol))
                e_tot.append(jnp.exp(gtot))

        kk = [_mm_nt(lhs_kk[c], k[c]) for c in chains]
        attn = [(kk[c][CHUNK:2 * CHUNK] * decay[c]).astype(BF16) for c in chains]
        lmat = [jnp.where(strict[c], kk[c][0:CHUNK] * decay[c], 0.0) for c in chains]
        t_inv = [eye - jnp.where(level_masks[0], lmat[c], 0.0) for c in chains]
        for off_mask in level_masks[1:]:
            tb = [t_inv[c].astype(BF16) for c in chains]
            w = [_mm(jnp.where(off_mask, lmat[c], 0.0), tb[c]) for c in chains]
            t_inv = [t_inv[c] - _mm(tb[c], w[c]) for c in chains]
        sol = [_mm(t_inv[c], rhs[c]) for c in chains]
        state = [state_ref[c] for c in chains]
        r = [_mm(jnp.concatenate([sol[c][:, HEAD_DIM:2 * HEAD_DIM].astype(BF16), q_dec[c]], axis=0), state[c])
             for c in chains]
        v_new = [(sol[c][:, 0:HEAD_DIM] - r[c][0:CHUNK]).astype(BF16) for c in chains]
        o = [r[c][CHUNK:2 * CHUNK] + _mm(attn[c], v_new[c]) for c in chains]
        upd = [_mm(k_dec[c].T, v_new[c]) for c in chains]
        for c in chains:
            d, h = divmod(c, N_HEADS)
            o_refs[d][pl.ds(rows[d], CHUNK), h * HEAD_DIM:(h + 1) * HEAD_DIM] = o[c]
            state_ref[c] = state[c] * e_tot[c] + upd[c]
        return carry

    lax.fori_loop(0, n_chunks, chunk_step, 0)


def _out_kernel(of_ref, ob_ref, za_ref, g0_ref, g1yb_ref, x_ref, dnw_ref, wdn_ref, wout_ref, fnw_ref,
                y_ref, *, final_norm):
    o = of_ref[...] + ob_ref[...]
    heads = []
    for h in range(N_HEADS):
        oh = o[:, h * HEAD_DIM:(h + 1) * HEAD_DIM]
        heads.append(oh * lax.rsqrt(jnp.mean(oh * oh, axis=-1, keepdims=True) + EPS) * dnw_ref[...])
    on = jnp.concatenate(heads, axis=1) * za_ref[...]
    y_a = _mm(on, wdn_ref[...])
    y = g0_ref[...] * y_a + g1yb_ref[...]
    out = x_ref[...] + _mm(y, wout_ref[...])
    if final_norm:
        out = out * lax.rsqrt(jnp.mean(out * out, axis=-1, keepdims=True) + EPS) * fnw_ref[...]
    y_ref[...] = out


def _const_spec(shape):
    nd = len(shape)
    return pl.BlockSpec(shape, lambda b, i: (0,) * nd, pipeline_mode=pl.Buffered(1))


def _compiler_params():
    return pltpu.CompilerParams(dimension_semantics=("arbitrary", "arbitrary"),
                                vmem_limit_bytes=VMEM_LIMIT_BYTES)


def _proj_call(x, nw, wconv, wrest, wabt, qkvw, cfw, cfb, lnw, lnb, wcf, gateb, arow, dtrow, acol, dtcol):
    bsz, seq, _ = x.shape
    tm = TOKEN_TILE
    n_tiles = seq // tm
    hb = tm // HALO
    n_hblocks = seq // HALO

    def tile3(width, dtype):
        return jax.ShapeDtypeStruct((bsz, seq, width), dtype), pl.BlockSpec((None, tm, width), lambda b, i: (b, i, 0))

    outs = [tile3(DN_QK, BF16), tile3(DN_QK, BF16), tile3(DN_V, BF16), tile3(DN_V, F32),
            tile3(D_MODEL, F32), tile3(D_MODEL, F32), tile3(2 * N_CHAIN, F32)]
    out_shapes = [o[0] for o in outs] + [jax.ShapeDtypeStruct((bsz, seq // CHUNK, 2 * N_CHAIN, CHUNK), F32)]
    out_specs = [o[1] for o in outs] + [pl.BlockSpec((None, tm // CHUNK, 2 * N_CHAIN, CHUNK),
                                                     lambda b, i: (b, i, 0, 0))]
    in_specs = [
        pl.BlockSpec((None, tm, D_MODEL), lambda b, i: (b, i, 0)),
        pl.BlockSpec((None, HALO, D_MODEL), lambda b, i: (b, jnp.maximum(i * hb - 1, 0), 0)),
        pl.BlockSpec((None, HALO, D_MODEL), lambda b, i: (b, jnp.minimum((i + 1) * hb, n_hblocks - 1), 0)),
    ] + [_const_spec(a.shape) for a in (nw, wconv, wrest, wabt, qkvw, cfw, cfb, lnw, lnb, wcf, gateb,
                                        arow, dtrow, acol, dtcol)]
    scratch = [
        pltpu.VMEM((tm + 2 * HALO, D_MODEL), BF16),
        pltpu.VMEM((tm + 2 * HALO, QKV + 2 * CF_WIDTH), F32),
        pltpu.VMEM((tm + 2 * HALO, CF_WIDTH), F32),
        pltpu.VMEM((SUBLANES - 1, tm + 2 * HALO - SUBLANES, CF_WIDTH), F32),
        pltpu.VMEM((tm, CF_WIDTH), BF16),
        pltpu.VMEM((tm, CF_WIDTH), F32),
    ]
    return pl.pallas_call(
        _proj_kernel, grid=(bsz, n_tiles), in_specs=in_specs, out_specs=out_specs, out_shape=out_shapes,
        scratch_shapes=scratch, compiler_params=_compiler_params(), name="proj_prep",
    )(x, x, x, nw, wconv, wrest, wabt, qkvw, cfw, cfb, lnw, lnb, wcf, gateb, arow, dtrow, acol, dtcol)


def _delta_call(q, k, v, gbtok, gblane):
    bsz, seq, _ = q.shape
    tm = TOKEN_TILE
    n_tiles = seq // tm
    fwd3 = lambda w: pl.BlockSpec((None, tm, w), lambda b, i: (b, i, 0))
    bwd3 = lambda w: pl.BlockSpec((None, tm, w), lambda b, i: (b, n_tiles - 1 - i, 0))
    lane_f = pl.BlockSpec((None, tm // CHUNK, 2 * N_CHAIN, CHUNK), lambda b, i: (b, i, 0, 0))
    lane_b = pl.BlockSpec((None, tm // CHUNK, 2 * N_CHAIN, CHUNK), lambda b, i: (b, n_tiles - 1 - i, 0, 0))
    in_specs = [fwd3(DN_QK), fwd3(DN_QK), fwd3(DN_V), bwd3(DN_QK), bwd3(DN_QK), bwd3(DN_V),
                fwd3(2 * N_CHAIN), bwd3(2 * N_CHAIN), lane_f, lane_b]
    out_shape = [jax.ShapeDtypeStruct((bsz, seq, DN_V), F32)] * 2
    return pl.pallas_call(
        _delta_kernel, grid=(bsz, n_tiles), in_specs=in_specs, out_specs=[fwd3(DN_V), bwd3(DN_V)],
        out_shape=out_shape, scratch_shapes=[pltpu.VMEM((N_CHAIN, HEAD_DIM, HEAD_DIM), F32)],
        compiler_params=_compiler_params(), name="delta_scan",
    )(q, k, v, q, k, v, gbtok, gbtok, gblane, gblane)


def _out_call(o_f, o_b, za, g0, g1yb, x, dnw, wdn, wout, fnw, final_norm):
    bsz, seq, _ = x.shape
    tm = TOKEN_TILE
    tile = lambda w: pl.BlockSpec((None, tm, w), lambda b, i: (b, i, 0))
    in_specs = [tile(DN_V), tile(DN_V), tile(DN_V), tile(D_MODEL), tile(D_MODEL), tile(D_MODEL)] + \
               [_const_spec(a.shape) for a in (dnw, wdn, wout, fnw)]
    return pl.pallas_call(
        functools.partial(_out_kernel, final_norm=final_norm), grid=(bsz, seq // tm), in_specs=in_specs,
        out_specs=tile(D_MODEL), out_shape=jax.ShapeDtypeStruct(x.shape, F32),
        compiler_params=_compiler_params(), name="merge_out",
    )(o_f, o_b, za, g0, g1yb, x, dnw, wdn, wout, fnw)


def kernel(x, norm_w, w_in, qkv_conv_w, a_log, dt_bias, dn_norm_w, w_dn_out, cf_conv_w, cf_conv_b,
           cf_ln_w, cf_ln_b, w_cf_out, gate_b, w_out, final_norm_w):
    depth = w_in.shape[0]
    bsz, seq, d_model = x.shape
    assert d_model == D_MODEL and seq % TOKEN_TILE == 0 and TOKEN_TILE % CHUNK == 0

    o_za = QKV
    o_ab = o_za + DN_V
    o_cf = o_ab + 2 * N_CHAIN
    o_zb = o_cf + 2 * CF_WIDTH
    o_gate = o_zb + CF_WIDTH

    for l in range(depth):
        w = w_in[l]
        wconv = jnp.concatenate([w[:, 0:QKV], w[:, o_cf:o_cf + 2 * CF_WIDTH]], axis=1).astype(BF16)
        w_ab = w[:, o_ab:o_ab + 2 * N_CHAIN]
        wrest = jnp.concatenate([w[:, o_za:o_za + DN_V], w[:, o_zb:o_zb + CF_WIDTH],
                                 w[:, o_gate:o_gate + 2 * D_MODEL],
                                 jnp.pad(w_ab, ((0, 0), (0, LANES - 2 * N_CHAIN)))], axis=1).astype(BF16)
        wabt = w_ab.T.astype(BF16)
        pad8 = lambda t: jnp.concatenate([t.reshape(-1), jnp.zeros((N_CHAIN,), F32)])
        a16, dt16 = pad8(a_log[l]), pad8(dt_bias[l])
        q, k, v, za, g0, g1yb, gbtok, gblane = _proj_call(
            x, norm_w[l][None, :], wconv, wrest, wabt, qkv_conv_w[l], cf_conv_w[l], cf_conv_b[l][None, :],
            cf_ln_w[l][None, :], cf_ln_b[l][None, :], w_cf_out[l].astype(BF16), gate_b[l][None, :],
            a16[None, :], dt16[None, :], a16[:, None], dt16[:, None])
        o_f, o_b = _delta_call(q, k, v, gbtok, gblane)
        x = _out_call(o_f, o_b, za, g0, g1yb, x, dn_norm_w[l][None, :], w_dn_out[l].astype(BF16),
                      w_out[l].astype(BF16), final_norm_w[None, :], final_norm=(l == depth - 1))
    return x
```

```python
import functools

import jax
import jax.numpy as jnp
from jax import lax
from jax.experimental import pallas as pl
from jax.experimental.pallas import tpu as pltpu

EPS = 1e-6
D_MODEL = 1024
N_HEADS = 4
HEAD_DIM = 128
DN_QK = N_HEADS * HEAD_DIM
DN_V = N_HEADS * HEAD_DIM
DN_CONV = 5
CF_WIDTH = 512
CF_CONV = 31
N_DIR = 2
N_CHAIN = N_DIR * N_HEADS
QKV = 2 * DN_QK + DN_V

LANES = 128
SUBLANES = 8
MXU_N = 256
V7X_VMEM_BYTES = 64 * 1024 * 1024
VMEM_LIMIT_BYTES = V7X_VMEM_BYTES - 6 * 1024 * 1024

TOKEN_TILE = 512
HALO = 16
CHUNK = 128
ROW_BLOCK = 64

BF16 = jnp.bfloat16
F32 = jnp.float32


def _sigmoid(x):
    return 1.0 / (1.0 + jnp.exp(-x))


def _silu(x):
    return x * _sigmoid(x)


def _softplus(x):
    return jnp.maximum(x, 0.0) + jnp.log1p(jnp.exp(-jnp.abs(x)))


def _mm(a, b):
    return jnp.dot(a.astype(BF16), b.astype(BF16), preferred_element_type=F32)


def _mm_nt(a, b):
    return lax.dot_general(a.astype(BF16), b.astype(BF16), (((1,), (1,)), ((), ())),
                           preferred_element_type=F32)


def _split3(x):
    hi = x.astype(BF16)
    r1 = x - hi.astype(F32)
    mid = r1.astype(BF16)
    lo = (r1 - mid.astype(F32)).astype(BF16)
    return hi, mid, lo


def _interleave(mxu_items, vpu_items):
    n, m = len(vpu_items), max(len(mxu_items), 1)
    done = 0
    for idx, item in enumerate(mxu_items):
        item()
        upto = (idx + 1) * n // m
        for unit in vpu_items[done:upto]:
            unit()
        done = upto
    for unit in vpu_items[done:]:
        unit()


def _proj_kernel(x_ref, xp_ref, xn_ref, nw_ref, wconv_ref, wrest_ref, wabt_ref, qkvw_ref,
                 cfw_ref, cfb_ref, lnw_ref, lnb_ref, wcf_ref, gateb_ref, acol_ref, dtcol_ref,
                 q_ref, k_ref, v_ref, za_ref, g0_ref, g1yb_ref, gbtok_ref, gblane_ref,
                 hall_ref, pq_ref, pk_ref, pv_ref, pcf_ref, u_ref, ush_ref, cv_ref, ub_ref, zb_ref, g1_ref):
    tm = x_ref.shape[0]
    rows_h = tm + 2 * HALO
    half_h = rows_h // 2
    half = tm // 2
    i = pl.program_id(1)
    n_tiles = pl.num_programs(1)
    nw = nw_ref[...]

    def rms(xv):
        return xv * lax.rsqrt(jnp.mean(xv * xv, axis=-1, keepdims=True) + EPS) * nw

    def conv_proj_items(dst_ref, c0, width):
        items = []
        for m0 in (0, half_h):
            for n0 in range(0, width, MXU_N):
                def item(m0=m0, n0=n0):
                    dst_ref[m0:m0 + half_h, n0:n0 + MXU_N] = jnp.dot(
                        hall_ref[m0:m0 + half_h, :], wconv_ref[:, c0 + n0:c0 + n0 + MXU_N],
                        preferred_element_type=F32)
                items.append(item)
        return items

    def rest_proj_items(c0, width, store):
        items = []
        for m0 in (0, half):
            for n0 in range(0, width, MXU_N):
                def item(m0=m0, n0=n0):
                    res = jnp.dot(hall_ref[HALO + m0:HALO + m0 + half, :],
                                  wrest_ref[:, c0 + n0:c0 + n0 + MXU_N], preferred_element_type=F32)
                    store(slice(m0, m0 + half), slice(n0, n0 + MXU_N), res)
                items.append(item)
        return items

    def store_za(rs, cs, res):
        za_ref[rs, cs] = _silu(res).astype(BF16)

    def store_zb(rs, cs, res):
        zb_ref[rs, cs] = res

    def store_g0(rs, cs, res):
        g0_ref[rs, cs] = _sigmoid(res + gateb_ref[:, cs]).astype(BF16)

    def store_g1(rs, cs, res):
        g1_ref[rs, cs] = _sigmoid(res + gateb_ref[:, D_MODEL + cs.start:D_MODEL + cs.stop])

    def branch_b_out_items(pt, n_parts):
        part = tm // n_parts
        rs = slice(pt * part, (pt + 1) * part)
        items = []
        for n0 in range(0, D_MODEL, MXU_N):
            def item(n0=n0):
                y_b = jnp.dot(ub_ref[rs, :], wcf_ref[:, n0:n0 + MXU_N], preferred_element_type=F32)
                g1yb_ref[rs, n0:n0 + MXU_N] = (g1_ref[rs, n0:n0 + MXU_N] * y_b).astype(BF16)
            items.append(item)
        return items

    def decay_items():
        def item():
            h_main = hall_ref[HALO:HALO + tm, :]
            ab_lane = lax.dot_general(wabt_ref[...], h_main, (((1,), (1,)), ((), ())),
                                      preferred_element_type=F32)
            g_lane = -jnp.exp(acol_ref[...]) * _softplus(ab_lane + dtcol_ref[...])
            beta_lane = _sigmoid(ab_lane)
            rid = lax.broadcasted_iota(jnp.int32, (CHUNK, CHUNK), 0)
            cid = lax.broadcasted_iota(jnp.int32, (CHUNK, CHUNK), 1)
            upper_b = (rid <= cid).astype(BF16)
            lower_b = (rid >= cid).astype(BF16)
            row_c = lax.broadcasted_iota(jnp.int32, (2 * N_CHAIN, CHUNK), 0)
            nr = 2 * N_CHAIN
            for c in range(tm // CHUNK):
                rows = slice(c * CHUNK, (c + 1) * CHUNK)
                parts = jnp.concatenate(_split3(g_lane[:, rows]), axis=0)
                pre = jnp.dot(parts, upper_b, preferred_element_type=F32)
                suf = jnp.dot(parts, lower_b, preferred_element_type=F32)
                pre = pre[0:nr] + pre[nr:2 * nr] + pre[2 * nr:3 * nr]
                suf = suf[0:nr] + suf[nr:2 * nr] + suf[2 * nr:3 * nr]
                blk = jnp.where(row_c < N_HEADS, pre, jnp.where(row_c < N_CHAIN, suf, beta_lane[:, rows]))
                gblane_ref[c] = blk
                gbtok_ref[rows, :] = blk.T
        return [item]

    def short_conv_units(src_ref, group, dst_ref):
        off5 = HALO - DN_CONV // 2
        units = []
        for rb in range(tm // ROW_BLOCK):
            for h in range(N_HEADS):
                def unit(r0=rb * ROW_BLOCK, c0=h * HEAD_DIM):
                    wc = group * DN_QK + c0
                    acc = None
                    for j in range(DN_CONV):
                        t = (src_ref[r0 + off5 + j:r0 + off5 + j + ROW_BLOCK, c0:c0 + HEAD_DIM]
                             * qkvw_ref[j:j + 1, wc:wc + HEAD_DIM])
                        acc = t if acc is None else acc + t
                    s = _silu(acc)
                    if group < 2:
                        s = s * lax.rsqrt(jnp.sum(s * s, axis=-1, keepdims=True) + EPS)
                    if group == 0:
                        s = s * (HEAD_DIM ** -0.5)
                    dst_ref[r0:r0 + ROW_BLOCK, c0:c0 + HEAD_DIM] = s.astype(BF16)
                units.append(unit)
        return units

    def glu_units():
        units = []
        n_split = 4
        rows = rows_h // n_split
        for p in range(n_split):
            def unit(r0=p * rows):
                u_ref[r0:r0 + rows, :] = (pcf_ref[r0:r0 + rows, 0:CF_WIDTH]
                                          * _sigmoid(pcf_ref[r0:r0 + rows, CF_WIDTH:2 * CF_WIDTH]))
            units.append(unit)
        sh_rows = ush_ref.shape[1]
        for s in range(1, SUBLANES):
            for r0, n in ((0, half_h), (half_h, sh_rows - half_h)):
                def unit(s=s, r0=r0, n=n):
                    ush_ref[s - 1, r0:r0 + n, :] = u_ref[s + r0:s + r0 + n, :]
                units.append(unit)
        return units

    def long_conv_units(pt, n_parts):
        off31 = HALO - CF_CONV // 2
        part = tm // n_parts
        units = []
        for rb in range(part // ROW_BLOCK):
            r0 = pt * part + rb * ROW_BLOCK
            for cg in range(CF_WIDTH // LANES):
                def unit(r0=r0, c0=cg * LANES):
                    acc = None
                    for j in range(CF_CONV):
                        s = (off31 + j) % SUBLANES
                        a0 = r0 + off31 + j - s
                        if s == 0:
                            tap = u_ref[a0:a0 + ROW_BLOCK, c0:c0 + LANES]
                        else:
                            tap = ush_ref[s - 1, a0:a0 + ROW_BLOCK, c0:c0 + LANES]
                        t = tap * cfw_ref[j:j + 1, c0:c0 + LANES]
                        acc = t if acc is None else acc + t
                    cv_ref[r0:r0 + ROW_BLOCK, c0:c0 + LANES] = acc
                units.append(unit)

            def norm_unit(r0=r0):
                cv = cv_ref[r0:r0 + ROW_BLOCK, :] + cfb_ref[...]
                mu = jnp.mean(cv, axis=-1, keepdims=True)
                xc = cv - mu
                y = xc * lax.rsqrt(jnp.mean(xc * xc, axis=-1, keepdims=True) + EPS) * lnw_ref[...] + lnb_ref[...]
                ub_ref[r0:r0 + ROW_BLOCK, :] = (_silu(y) * _silu(zb_ref[r0:r0 + ROW_BLOCK, :])).astype(BF16)
            units.append(norm_unit)
        return units

    hall_ref[0:HALO, :] = jnp.where(i > 0, rms(xp_ref[...]), 0.0).astype(BF16)
    hall_ref[HALO:HALO + half, :] = rms(x_ref[0:half, :]).astype(BF16)

    def rms_rest():
        hall_ref[HALO + half:HALO + tm, :] = rms(x_ref[half:tm, :]).astype(BF16)
        hall_ref[HALO + tm:, :] = jnp.where(i < n_tiles - 1, rms(xn_ref[...]), 0.0).astype(BF16)

    q_items = conv_proj_items(pq_ref, 0, DN_QK)
    n_first = len(q_items) // 2
    _interleave(q_items[:n_first], [rms_rest])
    _interleave(q_items[n_first:] + conv_proj_items(pk_ref, DN_QK, DN_QK), short_conv_units(pq_ref, 0, q_ref))
    _interleave(conv_proj_items(pv_ref, 2 * DN_QK, DN_V), short_conv_units(pk_ref, 1, k_ref))
    _interleave(conv_proj_items(pcf_ref, QKV, 2 * CF_WIDTH), short_conv_units(pv_ref, 2, v_ref))
    _interleave(rest_proj_items(0, DN_V, store_za) + rest_proj_items(DN_V, CF_WIDTH, store_zb) + decay_items(),
                glu_units())
    g_off = DN_V + CF_WIDTH
    _interleave(rest_proj_items(g_off + D_MODEL, D_MODEL, store_g1), long_conv_units(0, 2))
    _interleave(rest_proj_items(g_off, D_MODEL, store_g0) + branch_b_out_items(0, 2), long_conv_units(1, 2))
    _interleave(branch_b_out_items(1, 2), [])


def _delta_kernel(qf_ref, kf_ref, vf_ref, qb_ref, kb_ref, vb_ref, gtf_ref, gtb_ref, glf_ref, glb_ref,
                  of_ref, ob_ref, state_ref):
    tm = qf_ref.shape[0]
    n_chunks = tm // CHUNK
    i = pl.program_id(1)

    @pl.when(i == 0)
    def _():
        state_ref[...] = jnp.zeros_like(state_ref)

    rid = lax.broadcasted_iota(jnp.int32, (CHUNK, CHUNK), 0)
    cid = lax.broadcasted_iota(jnp.int32, (CHUNK, CHUNK), 1)
    lower = rid >= cid
    upper = rid <= cid
    eye = (rid == cid).astype(F32)
    chains = range(N_CHAIN)
    same_block = [(rid >> j) == (cid >> j) for j in range(1, CHUNK.bit_length())]
    level_masks = [same_block[0]] + [same_block[j] & jnp.logical_not(same_block[j - 1])
                                     for j in range(1, len(same_block))]

    def chunk_step(n, carry):
        rows = (pl.multiple_of(n * CHUNK, CHUNK), pl.multiple_of((n_chunks - 1 - n) * CHUNK, CHUNK))
        cidx = (n, n_chunks - 1 - n)
        gt_refs = (gtf_ref, gtb_ref)
        gl_refs = (glf_ref, glb_ref)
        q_refs = (qf_ref, qb_ref)
        k_refs = (kf_ref, kb_ref)
        v_refs = (vf_ref, vb_ref)
        o_refs = (of_ref, ob_ref)

        k, decay, strict, rhs, q_dec, k_dec, e_tot, lhs_kk = [], [], [], [], [], [], [], []
        for d in range(N_DIR):
            r0 = rows[d]
            gcum_tok = gt_refs[d][pl.ds(r0, CHUNK), :]
            gcum_lane = gl_refs[d][cidx[d]]
            if d == 0:
                mask, strict_d = lower, rid > cid
            else:
                mask, strict_d = upper, rid < cid
            for h in range(N_HEADS):
                c = d * N_HEADS + h
                hs = slice(h * HEAD_DIM, (h + 1) * HEAD_DIM)
                q_c = q_refs[d][pl.ds(r0, CHUNK), hs]
                k_c = k_refs[d][pl.ds(r0, CHUNK), hs]
                v_c = v_refs[d][pl.ds(r0, CHUNK), hs].astype(F32)
                kf_c = k_c.astype(F32)
                gcol = gcum_tok[:, c:c + 1]
                grow = gcum_lane[c:c + 1, :]
                beta = gcum_tok[:, N_CHAIN + c:N_CHAIN + c + 1]
                gtot = gcum_lane[c:c + 1, CHUNK - 1:CHUNK] if d == 0 else gcum_lane[c:c + 1, 0:1]
                k_beta = kf_c * beta
                e_col = jnp.exp(gcol)
                k.append(k_c)
                strict.append(strict_d)
                decay.append(jnp.where(mask, jnp.exp(jnp.where(mask, gcol - grow, 0.0)), 0.0))
                lhs_kk.append(jnp.concatenate([k_beta.astype(BF16), q_c], axis=0))
                rhs.append(jnp.concatenate([v_c * beta, k_beta * e_col], axis=1).astype(BF16))
                q_dec.append((q_c.astype(F32) * e_col).astype(BF16))
                k_dec.append(kf_c * jnp.exp(gtot - gcol))
                e_tot.append(jnp.exp(gtot))

        kk = [_mm_nt(lhs_kk[c], k[c]) for c in chains]
        attn = [(kk[c][CHUNK:2 * CHUNK] * decay[c]).astype(BF16) for c in chains]
        lmat = [jnp.where(strict[c], kk[c][0:CHUNK] * decay[c], 0.0) for c in chains]
        t_inv = [eye - jnp.where(level_masks[0], lmat[c], 0.0) for c in chains]
        for off_mask in level_masks[1:]:
            tb = [t_inv[c].astype(BF16) for c in chains]
            w = [_mm(jnp.where(off_mask, lmat[c], 0.0), tb[c]) for c in chains]
            t_inv = [t_inv[c] - _mm(tb[c], w[c]) for c in chains]
        sol = [_mm(t_inv[c], rhs[c]) for c in chains]
        state = [state_ref[c] for c in chains]
        r = [_mm(jnp.concatenate([sol[c][:, HEAD_DIM:2 * HEAD_DIM].astype(BF16), q_dec[c]], axis=0), state[c])
             for c in chains]
        v_new = [(sol[c][:, 0:HEAD_DIM] - r[c][0:CHUNK]).astype(BF16) for c in chains]
        o = [r[c][CHUNK:2 * CHUNK] + _mm(attn[c], v_new[c]) for c in chains]
        upd = [_mm(k_dec[c].T, v_new[c]) for c in chains]
        for c in chains:
            d, h = divmod(c, N_HEADS)
            o_refs[d][pl.ds(rows[d], CHUNK), h * HEAD_DIM:(h + 1) * HEAD_DIM] = o[c].astype(BF16)
            state_ref[c] = state[c] * e_tot[c] + upd[c]
        return carry

    lax.fori_loop(0, n_chunks, chunk_step, 0)


def _out_kernel(of_ref, ob_ref, za_ref, g0_ref, g1yb_ref, x_ref, dnw_ref, wdn_ref, wout_ref, fnw_ref,
                y_ref, *, final_norm):
    o = of_ref[...].astype(F32) + ob_ref[...].astype(F32)
    heads = []
    for h in range(N_HEADS):
        oh = o[:, h * HEAD_DIM:(h + 1) * HEAD_DIM]
        heads.append(oh * lax.rsqrt(jnp.mean(oh * oh, axis=-1, keepdims=True) + EPS) * dnw_ref[...])
    on = jnp.concatenate(heads, axis=1) * za_ref[...].astype(F32)
    y_a = _mm(on, wdn_ref[...])
    y = g0_ref[...].astype(F32) * y_a + g1yb_ref[...].astype(F32)
    out = x_ref[...] + _mm(y, wout_ref[...])
    if final_norm:
        out = out * lax.rsqrt(jnp.mean(out * out, axis=-1, keepdims=True) + EPS) * fnw_ref[...]
    y_ref[...] = out


def _const_spec(shape):
    nd = len(shape)
    return pl.BlockSpec(shape, lambda b, i: (0,) * nd, pipeline_mode=pl.Buffered(1))


def _compiler_params():
    return pltpu.CompilerParams(dimension_semantics=("arbitrary", "arbitrary"),
                                vmem_limit_bytes=VMEM_LIMIT_BYTES)


def _proj_call(x, nw, wconv, wrest, wabt, qkvw, cfw, cfb, lnw, lnb, wcf, gateb, acol, dtcol):
    bsz, seq, _ = x.shape
    tm = TOKEN_TILE
    n_tiles = seq // tm
    hb = tm // HALO
    n_hblocks = seq // HALO

    def tile3(width, dtype):
        return jax.ShapeDtypeStruct((bsz, seq, width), dtype), pl.BlockSpec((None, tm, width), lambda b, i: (b, i, 0))

    outs = [tile3(DN_QK, BF16), tile3(DN_QK, BF16), tile3(DN_V, BF16), tile3(DN_V, BF16),
            tile3(D_MODEL, BF16), tile3(D_MODEL, BF16), tile3(2 * N_CHAIN, F32)]
    out_shapes = [o[0] for o in outs] + [jax.ShapeDtypeStruct((bsz, seq // CHUNK, 2 * N_CHAIN, CHUNK), F32)]
    out_specs = [o[1] for o in outs] + [pl.BlockSpec((None, tm // CHUNK, 2 * N_CHAIN, CHUNK),
                                                     lambda b, i: (b, i, 0, 0))]
    in_specs = [
        pl.BlockSpec((None, tm, D_MODEL), lambda b, i: (b, i, 0)),
        pl.BlockSpec((None, HALO, D_MODEL), lambda b, i: (b, jnp.maximum(i * hb - 1, 0), 0)),
        pl.BlockSpec((None, HALO, D_MODEL), lambda b, i: (b, jnp.minimum((i + 1) * hb, n_hblocks - 1), 0)),
    ] + [_const_spec(a.shape) for a in (nw, wconv, wrest, wabt, qkvw, cfw, cfb, lnw, lnb, wcf, gateb,
                                        acol, dtcol)]
    rows_h = tm + 2 * HALO
    scratch = [
        pltpu.VMEM((rows_h, D_MODEL), BF16),
        pltpu.VMEM((rows_h, DN_QK), F32),
        pltpu.VMEM((rows_h, DN_QK), F32),
        pltpu.VMEM((rows_h, DN_V), F32),
        pltpu.VMEM((rows_h, 2 * CF_WIDTH), F32),
        pltpu.VMEM((rows_h, CF_WIDTH), F32),
        pltpu.VMEM((SUBLANES - 1, rows_h - SUBLANES, CF_WIDTH), F32),
        pltpu.VMEM((tm, CF_WIDTH), F32),
        pltpu.VMEM((tm, CF_WIDTH), BF16),
        pltpu.VMEM((tm, CF_WIDTH), F32),
        pltpu.VMEM((tm, D_MODEL), F32),
    ]
    return pl.pallas_call(
        _proj_kernel, grid=(bsz, n_tiles), in_specs=in_specs, out_specs=out_specs, out_shape=out_shapes,
        scratch_shapes=scratch, compiler_params=_compiler_params(), name="proj_prep",
    )(x, x, x, nw, wconv, wrest, wabt, qkvw, cfw, cfb, lnw, lnb, wcf, gateb, acol, dtcol)


def _delta_call(q, k, v, gbtok, gblane):
    bsz, seq, _ = q.shape
    tm = TOKEN_TILE
    n_tiles = seq // tm
    fwd3 = lambda w: pl.BlockSpec((None, tm, w), lambda b, i: (b, i, 0))
    bwd3 = lambda w: pl.BlockSpec((None, tm, w), lambda b, i: (b, n_tiles - 1 - i, 0))
    lane_f = pl.BlockSpec((None, tm // CHUNK, 2 * N_CHAIN, CHUNK), lambda b, i: (b, i, 0, 0))
    lane_b = pl.BlockSpec((None, tm // CHUNK, 2 * N_CHAIN, CHUNK), lambda b, i: (b, n_tiles - 1 - i, 0, 0))
    in_specs = [fwd3(DN_QK), fwd3(DN_QK), fwd3(DN_V), bwd3(DN_QK), bwd3(DN_QK), bwd3(DN_V),
                fwd3(2 * N_CHAIN), bwd3(2 * N_CHAIN), lane_f, lane_b]
    out_shape = [jax.ShapeDtypeStruct((bsz, seq, DN_V), BF16)] * 2
    return pl.pallas_call(
        _delta_kernel, grid=(bsz, n_tiles), in_specs=in_specs, out_specs=[fwd3(DN_V), bwd3(DN_V)],
        out_shape=out_shape, scratch_shapes=[pltpu.VMEM((N_CHAIN, HEAD_DIM, HEAD_DIM), F32)],
        compiler_params=_compiler_params(), name="delta_scan",
    )(q, k, v, q, k, v, gbtok, gbtok, gblane, gblane)


def _out_call(o_f, o_b, za, g0, g1yb, x, dnw, wdn, wout, fnw, final_norm):
    bsz, seq, _ = x.shape
    tm = TOKEN_TILE
    tile = lambda w: pl.BlockSpec((None, tm, w), lambda b, i: (b, i, 0))
    in_specs = [tile(DN_V), tile(DN_V), tile(DN_V), tile(D_MODEL), tile(D_MODEL), tile(D_MODEL)] + \
               [_const_spec(a.shape) for a in (dnw, wdn, wout, fnw)]
    return pl.pallas_call(
        functools.partial(_out_kernel, final_norm=final_norm), grid=(bsz, seq // tm), in_specs=in_specs,
        out_specs=tile(D_MODEL), out_shape=jax.ShapeDtypeStruct(x.shape, F32),
        compiler_params=_compiler_params(), name="merge_out",
    )(o_f, o_b, za, g0, g1yb, x, dnw, wdn, wout, fnw)


def kernel(x, norm_w, w_in, qkv_conv_w, a_log, dt_bias, dn_norm_w, w_dn_out, cf_conv_w, cf_conv_b,
           cf_ln_w, cf_ln_b, w_cf_out, gate_b, w_out, final_norm_w):
    depth = w_in.shape[0]
    bsz, seq, d_model = x.shape
    assert d_model == D_MODEL and seq % TOKEN_TILE == 0 and TOKEN_TILE % CHUNK == 0

    o_za = QKV
    o_ab = o_za + DN_V
    o_cf = o_ab + 2 * N_CHAIN
    o_zb = o_cf + 2 * CF_WIDTH
    o_gate = o_zb + CF_WIDTH

    for l in range(depth):
        w = w_in[l]
        wconv = jnp.concatenate([w[:, 0:QKV], w[:, o_cf:o_cf + 2 * CF_WIDTH]], axis=1).astype(BF16)
        w_ab = w[:, o_ab:o_ab + 2 * N_CHAIN]
        wrest = jnp.concatenate([w[:, o_za:o_za + DN_V], w[:, o_zb:o_zb + CF_WIDTH],
                                 w[:, o_gate:o_gate + 2 * D_MODEL]], axis=1).astype(BF16)
        wabt = w_ab.T.astype(BF16)
        pad8 = lambda t: jnp.concatenate([t.reshape(-1), jnp.zeros((N_CHAIN,), F32)])
        a16, dt16 = pad8(a_log[l]), pad8(dt_bias[l])
        q, k, v, za, g0, g1yb, gbtok, gblane = _proj_call(
            x, norm_w[l][None, :], wconv, wrest, wabt, qkv_conv_w[l], cf_conv_w[l], cf_conv_b[l][None, :],
            cf_ln_w[l][None, :], cf_ln_b[l][None, :], w_cf_out[l].astype(BF16), gate_b[l][None, :],
            a16[:, None], dt16[:, None])
        o_f, o_b = _delta_call(q, k, v, gbtok, gblane)
        x = _out_call(o_f, o_b, za, g0, g1yb, x, dn_norm_w[l][None, :], w_dn_out[l].astype(BF16),
                      w_out[l].astype(BF16), final_norm_w[None, :], final_norm=(l == depth - 1))
    return x
```

```python
import functools

import jax
import jax.numpy as jnp
from jax import lax
from jax.experimental import pallas as pl
from jax.experimental.pallas import tpu as pltpu

EPS = 1e-6
D_MODEL = 1024
N_HEADS = 4
HEAD_DIM = 128
DN_QK = N_HEADS * HEAD_DIM
DN_V = N_HEADS * HEAD_DIM
DN_CONV = 5
CF_WIDTH = 512
CF_CONV = 31
N_DIR = 2
N_CHAIN = N_DIR * N_HEADS
QKV = 2 * DN_QK + DN_V

LANES = 128
SUBLANES = 8
MXU_N = 256
V7X_VMEM_BYTES = 64 * 1024 * 1024
VMEM_LIMIT_BYTES = V7X_VMEM_BYTES - 6 * 1024 * 1024

TOKEN_TILE = 512
HALO = 16
CHUNK = 128
ROW_BLOCK = 64
TOKEN_ROWS = 16

BF16 = jnp.bfloat16
F32 = jnp.float32


def _sigmoid(x):
    return 1.0 / (1.0 + jnp.exp(-x))


def _silu(x):
    return x * _sigmoid(x)


def _softplus(x):
    return jnp.maximum(x, 0.0) + jnp.log1p(jnp.exp(-jnp.abs(x)))


def _mm(a, b):
    return jnp.dot(a.astype(BF16), b.astype(BF16), preferred_element_type=F32)


def _mm_nt(a, b):
    return lax.dot_general(a.astype(BF16), b.astype(BF16), (((1,), (1,)), ((), ())),
                           preferred_element_type=F32)


def _split3(x):
    hi = x.astype(BF16)
    r1 = x - hi.astype(F32)
    mid = r1.astype(BF16)
    lo = (r1 - mid.astype(F32)).astype(BF16)
    return hi, mid, lo


def _plan(mxu_items, vpu_items):
    plan = []
    n, m = len(vpu_items), max(len(mxu_items), 1)
    done = 0
    for idx, item in enumerate(mxu_items):
        plan.append(("mxu", item))
        upto = (idx + 1) * n // m
        plan.extend(("vpu", unit) for unit in vpu_items[done:upto])
        done = upto
    plan.extend(("vpu", unit) for unit in vpu_items[done:])
    return plan


def _run_plan(plan):
    unit_tokens = []
    piece_tokens = []
    last_unit = None
    gate_next_unit = False
    for kind, fn in plan:
        if kind == "mxu":
            j = len(piece_tokens)
            unit_tokens.append(last_unit)
            piece_tokens.append(fn(unit_tokens[j - 2] if j >= 2 else None))
            gate_next_unit = True
        else:
            j = len(piece_tokens)
            gate = piece_tokens[j - 2] if gate_next_unit and j >= 2 else None
            last_unit = fn(gate)
            gate_next_unit = False


def _proj_kernel(x_ref, xp_ref, xn_ref, nw_ref, wconv_ref, wrest_ref, wabt_ref, qkvw_ref,
                 cfw_ref, cfb_ref, lnw_ref, lnb_ref, wcf_ref, gateb_ref, acol_ref, dtcol_ref,
                 q_ref, k_ref, v_ref, za_ref, g0_ref, g1yb_ref, gbtok_ref, gblane_ref,
                 hall_ref, pq_ref, pk_ref, pv_ref, pcf_ref, u_ref, cv_ref, ub_ref, zb_ref, g1_ref):
    tm = x_ref.shape[0]
    rows_h = tm + 2 * HALO
    half_h = rows_h // 2
    half = tm // 2
    i = pl.program_id(1)
    n_tiles = pl.num_programs(1)
    nw = nw_ref[...]

    def rms(xv):
        return xv * lax.rsqrt(jnp.mean(xv * xv, axis=-1, keepdims=True) + EPS) * nw

    never = pl.program_id(0) < 0

    def wait_for(token, ref, row0, col0=0):
        if token is not None:
            if ref.dtype == BF16:
                rows, other = TOKEN_ROWS, token
            else:
                rows, other = SUBLANES, token[0:SUBLANES, :].astype(F32)
            blk = ref[row0:row0 + rows, col0:col0 + LANES]
            ref[row0:row0 + rows, col0:col0 + LANES] = jnp.where(never, other, blk)

    def token_of(val):
        return val[0:TOKEN_ROWS, 0:LANES].astype(BF16)

    def conv_proj_items(dst_ref, c0, width, per_head=False):
        items = []
        for m0 in (0, half_h):
            for n0 in range(0, width, MXU_N):
                def item(gate, m0=m0, n0=n0):
                    wait_for(gate, hall_ref, m0)
                    res = jnp.dot(hall_ref[m0:m0 + half_h, :], wconv_ref[:, c0 + n0:c0 + n0 + MXU_N],
                                  preferred_element_type=F32)
                    if per_head:
                        for cc in range(0, MXU_N, HEAD_DIM):
                            dst_ref[(n0 + cc) // HEAD_DIM, m0:m0 + half_h, :] = res[:, cc:cc + HEAD_DIM]
                    else:
                        dst_ref[m0:m0 + half_h, n0:n0 + MXU_N] = res
                    return token_of(res)
                items.append(item)
        return items

    def rest_proj_items(c0, width, store):
        items = []
        for m0 in (0, half):
            for n0 in range(0, width, MXU_N):
                def item(gate, m0=m0, n0=n0):
                    wait_for(gate, hall_ref, HALO + m0)
                    res = jnp.dot(hall_ref[HALO + m0:HALO + m0 + half, :],
                                  wrest_ref[:, c0 + n0:c0 + n0 + MXU_N], preferred_element_type=F32)
                    store(slice(m0, m0 + half), slice(n0, n0 + MXU_N), res)
                    return token_of(res)
                items.append(item)
        return items

    def store_za(rs, cs, res):
        za_ref[rs, cs] = _silu(res).astype(BF16)

    def store_zb(rs, cs, res):
        zb_ref[rs, cs] = res

    def store_g0(rs, cs, res):
        g0_ref[rs, cs] = _sigmoid(res + gateb_ref[:, cs]).astype(BF16)

    def store_g1(rs, cs, res):
        g1_ref[rs, cs] = _sigmoid(res + gateb_ref[:, D_MODEL + cs.start:D_MODEL + cs.stop])

    def branch_b_out_items(pt, n_parts):
        part = tm // n_parts
        rs = slice(pt * part, (pt + 1) * part)
        items = []
        for n0 in range(0, D_MODEL, MXU_N):
            def item(gate, n0=n0):
                y_b = jnp.dot(ub_ref[rs, :], wcf_ref[:, n0:n0 + MXU_N], preferred_element_type=F32)
                g1yb_ref[rs, n0:n0 + MXU_N] = (g1_ref[rs, n0:n0 + MXU_N] * y_b).astype(BF16)
                return token_of(y_b)
            items.append(item)
        return items

    def decay_items():
        def item(gate):
            wait_for(gate, hall_ref, HALO)
            h_main = hall_ref[HALO:HALO + tm, :]
            ab_lane = lax.dot_general(wabt_ref[...], h_main, (((1,), (1,)), ((), ())),
                                      preferred_element_type=F32)
            g_lane = -jnp.exp(acol_ref[...]) * _softplus(ab_lane + dtcol_ref[...])
            beta_lane = _sigmoid(ab_lane)
            rid = lax.broadcasted_iota(jnp.int32, (CHUNK, CHUNK), 0)
            cid = lax.broadcasted_iota(jnp.int32, (CHUNK, CHUNK), 1)
            upper_b = (rid <= cid).astype(BF16)
            lower_b = (rid >= cid).astype(BF16)
            row_c = lax.broadcasted_iota(jnp.int32, (2 * N_CHAIN, CHUNK), 0)
            nr = 2 * N_CHAIN
            for c in range(tm // CHUNK):
                rows = slice(c * CHUNK, (c + 1) * CHUNK)
                parts = jnp.concatenate(_split3(g_lane[:, rows]), axis=0)
                pre = jnp.dot(parts, upper_b, preferred_element_type=F32)
                suf = jnp.dot(parts, lower_b, preferred_element_type=F32)
                pre = pre[0:nr] + pre[nr:2 * nr] + pre[2 * nr:3 * nr]
                suf = suf[0:nr] + suf[nr:2 * nr] + suf[2 * nr:3 * nr]
                blk = jnp.where(row_c < N_HEADS, pre, jnp.where(row_c < N_CHAIN, suf, beta_lane[:, rows]))
                gblane_ref[c] = blk
                gbtok_ref[rows, :] = blk.T
            return token_of(ab_lane)
        return [item]

    def short_conv_units(src_ref, group, dst_ref):
        off5 = HALO - DN_CONV // 2
        units = []
        for rb in range(tm // ROW_BLOCK):
            for h in range(N_HEADS):
                def unit(gate, r0=rb * ROW_BLOCK, c0=h * HEAD_DIM, src=src_ref.at[h]):
                    wait_for(gate, src, r0 + HALO)
                    wc = group * DN_QK + c0
                    acc = None
                    for j in range(DN_CONV):
                        t = (src[r0 + off5 + j:r0 + off5 + j + ROW_BLOCK, :]
                             * qkvw_ref[j:j + 1, wc:wc + HEAD_DIM])
                        acc = t if acc is None else acc + t
                    s = _silu(acc)
                    if group < 2:
                        s = s * lax.rsqrt(jnp.sum(s * s, axis=-1, keepdims=True) + EPS)
                    if group == 0:
                        s = s * (HEAD_DIM ** -0.5)
                    out = s.astype(BF16)
                    dst_ref[r0:r0 + ROW_BLOCK, c0:c0 + HEAD_DIM] = out
                    return token_of(out)
                units.append(unit)
        return units

    def glu_units():
        units = []
        n_split = 4
        rows = rows_h // n_split
        for p in range(n_split):
            def unit(gate, r0=p * rows):
                wait_for(gate, pcf_ref, r0)
                val = (pcf_ref[r0:r0 + rows, 0:CF_WIDTH]
                       * _sigmoid(pcf_ref[r0:r0 + rows, CF_WIDTH:2 * CF_WIDTH]))
                for cg in range(CF_WIDTH // LANES):
                    u_ref[cg, r0:r0 + rows, :] = val[:, cg * LANES:(cg + 1) * LANES]
                return token_of(val)
            units.append(unit)
        return units

    def long_conv_units(pt, n_parts):
        off31 = HALO - CF_CONV // 2
        part = tm // n_parts
        units = []
        for rb in range(part // ROW_BLOCK):
            r0 = pt * part + rb * ROW_BLOCK
            for cg in range(CF_WIDTH // LANES):
                def unit(gate, r0=r0, c0=cg * LANES, src=u_ref.at[cg]):
                    wait_for(gate, src, r0 + HALO)
                    acc = None
                    for j in range(CF_CONV):
                        t = src[r0 + off31 + j:r0 + off31 + j + ROW_BLOCK, :] * cfw_ref[j:j + 1, c0:c0 + LANES]
                        acc = t if acc is None else acc + t
                    cv_ref[r0:r0 + ROW_BLOCK, c0:c0 + LANES] = acc
                    return token_of(acc)
                units.append(unit)

            def norm_unit(gate, r0=r0):
                wait_for(gate, cv_ref, r0)
                cv = cv_ref[r0:r0 + ROW_BLOCK, :] + cfb_ref[...]
                mu = jnp.mean(cv, axis=-1, keepdims=True)
                xc = cv - mu
                y = xc * lax.rsqrt(jnp.mean(xc * xc, axis=-1, keepdims=True) + EPS) * lnw_ref[...] + lnb_ref[...]
                out = (_silu(y) * _silu(zb_ref[r0:r0 + ROW_BLOCK, :])).astype(BF16)
                ub_ref[r0:r0 + ROW_BLOCK, :] = out
                return token_of(out)
            units.append(norm_unit)
        return units

    hall_ref[0:HALO, :] = jnp.where(i > 0, rms(xp_ref[...]), 0.0).astype(BF16)
    hall_ref[HALO:HALO + half, :] = rms(x_ref[0:half, :]).astype(BF16)

    def rms_rest(gate):
        hall_ref[HALO + half:HALO + tm, :] = rms(x_ref[half:tm, :]).astype(BF16)
        tail = jnp.where(i < n_tiles - 1, rms(xn_ref[...]), 0.0).astype(BF16)
        hall_ref[HALO + tm:, :] = tail
        return token_of(tail)

    q_items = conv_proj_items(pq_ref, 0, DN_QK, per_head=True)
    n_first = len(q_items) // 2
    g_off = DN_V + CF_WIDTH
    _run_plan(
        _plan(q_items[:n_first], [rms_rest])
        + _plan(q_items[n_first:] + conv_proj_items(pk_ref, DN_QK, DN_QK, per_head=True),
                short_conv_units(pq_ref, 0, q_ref))
        + _plan(conv_proj_items(pv_ref, 2 * DN_QK, DN_V, per_head=True), short_conv_units(pk_ref, 1, k_ref))
        + _plan(conv_proj_items(pcf_ref, QKV, 2 * CF_WIDTH), short_conv_units(pv_ref, 2, v_ref))
        + _plan(rest_proj_items(0, DN_V, store_za) + rest_proj_items(DN_V, CF_WIDTH, store_zb) + decay_items(),
                glu_units())
        + _plan(rest_proj_items(g_off + D_MODEL, D_MODEL, store_g1), long_conv_units(0, 2))
        + _plan(rest_proj_items(g_off, D_MODEL, store_g0) + branch_b_out_items(0, 2), long_conv_units(1, 2))
        + _plan(branch_b_out_items(1, 2), []))


def _delta_kernel(qf_ref, kf_ref, vf_ref, qb_ref, kb_ref, vb_ref, gtf_ref, gtb_ref, glf_ref, glb_ref,
                  of_ref, ob_ref, state_ref):
    tm = qf_ref.shape[0]
    n_chunks = tm // CHUNK
    i = pl.program_id(1)

    @pl.when(i == 0)
    def _():
        state_ref[...] = jnp.zeros_like(state_ref)

    rid = lax.broadcasted_iota(jnp.int32, (CHUNK, CHUNK), 0)
    cid = lax.broadcasted_iota(jnp.int32, (CHUNK, CHUNK), 1)
    lower = rid >= cid
    upper = rid <= cid
    eye = (rid == cid).astype(F32)
    chains = range(N_CHAIN)
    same_block = [(rid >> j) == (cid >> j) for j in range(1, CHUNK.bit_length())]
    level_masks = [same_block[0]] + [same_block[j] & jnp.logical_not(same_block[j - 1])
                                     for j in range(1, len(same_block))]

    def chunk_step(n, carry):
        rows = (pl.multiple_of(n * CHUNK, CHUNK), pl.multiple_of((n_chunks - 1 - n) * CHUNK, CHUNK))
        cidx = (n, n_chunks - 1 - n)
        gt_refs = (gtf_ref, gtb_ref)
        gl_refs = (glf_ref, glb_ref)
        q_refs = (qf_ref, qb_ref)
        k_refs = (kf_ref, kb_ref)
        v_refs = (vf_ref, vb_ref)
        o_refs = (of_ref, ob_ref)

        k, decay, strict, rhs, q_dec, k_dec, e_tot, lhs_kk = [], [], [], [], [], [], [], []
        for d in range(N_DIR):
            r0 = rows[d]
            gcum_tok = gt_refs[d][pl.ds(r0, CHUNK), :]
            gcum_lane = gl_refs[d][cidx[d]]
            if d == 0:
                mask, strict_d = lower, rid > cid
            else:
                mask, strict_d = upper, rid < cid
            for h in range(N_HEADS):
                c = d * N_HEADS + h
                hs = slice(h * HEAD_DIM, (h + 1) * HEAD_DIM)
                q_c = q_refs[d][pl.ds(r0, CHUNK), hs]
                k_c = k_refs[d][pl.ds(r0, CHUNK), hs]
                v_c = v_refs[d][pl.ds(r0, CHUNK), hs].astype(F32)
                kf_c = k_c.astype(F32)
                gcol = gcum_tok[:, c:c + 1]
                grow = gcum_lane[c:c + 1, :]
                beta = gcum_tok[:, N_CHAIN + c:N_CHAIN + c + 1]
                gtot = gcum_lane[c:c + 1, CHUNK - 1:CHUNK] if d == 0 else gcum_lane[c:c + 1, 0:1]
                k_beta = kf_c * beta
                e_col = jnp.exp(gcol)
                k.append(k_c)
                strict.append(strict_d)
                decay.append(jnp.where(mask, jnp.exp(jnp.where(mask, gcol - grow, 0.0)), 0.0))
                lhs_kk.append(jnp.concatenate([k_beta.astype(BF16), q_c], axis=0))
                rhs.append(jnp.concatenate([v_c * beta, k_beta * e_col], axis=1).astype(BF16))
                q_dec.append((q_c.astype(F32) * e_col).astype(BF16))
                k_dec.append(kf_c * jnp.exp(gtot - gcol))
                e_tot.append(jnp.exp(gtot))

        kk = [_mm_nt(lhs_kk[c], k[c]) for c in chains]
        attn = [(kk[c][CHUNK:2 * CHUNK] * decay[c]).astype(BF16) for c in chains]
        lmat = [jnp.where(strict[c], kk[c][0:CHUNK] * decay[c], 0.0) for c in chains]
        t_inv = [eye - jnp.where(level_masks[0], lmat[c], 0.0) for c in chains]
        for off_mask in level_masks[1:]:
            tb = [t_inv[c].astype(BF16) for c in chains]
            w = [_mm(jnp.where(off_mask, lmat[c], 0.0), tb[c]) for c in chains]
            t_inv = [t_inv[c] - _mm(tb[c], w[c]) for c in chains]
        sol = [_mm(t_inv[c], rhs[c]) for c in chains]
        state = [state_ref[c] for c in chains]
        r = [_mm(jnp.concatenate([sol[c][:, HEAD_DIM:2 * HEAD_DIM].astype(BF16), q_dec[c]], axis=0), state[c])
             for c in chains]
        v_new = [(sol[c][:, 0:HEAD_DIM] - r[c][0:CHUNK]).astype(BF16) for c in chains]
        o = [r[c][CHUNK:2 * CHUNK] + _mm(attn[c], v_new[c]) for c in chains]
        upd = [_mm(k_dec[c].T, v_new[c]) for c in chains]
        for c in chains:
            d, h = divmod(c, N_HEADS)
            o_refs[d][pl.ds(rows[d], CHUNK), h * HEAD_DIM:(h + 1) * HEAD_DIM] = o[c].astype(BF16)
            state_ref[c] = state[c] * e_tot[c] + upd[c]
        return carry

    lax.fori_loop(0, n_chunks, chunk_step, 0)


def _out_kernel(of_ref, ob_ref, za_ref, g0_ref, g1yb_ref, x_ref, dnw_ref, wdn_ref, wout_ref, fnw_ref,
                y_ref, *, final_norm):
    o = of_ref[...].astype(F32) + ob_ref[...].astype(F32)
    heads = []
    for h in range(N_HEADS):
        oh = o[:, h * HEAD_DIM:(h + 1) * HEAD_DIM]
        heads.append(oh * lax.rsqrt(jnp.mean(oh * oh, axis=-1, keepdims=True) + EPS) * dnw_ref[...])
    on = jnp.concatenate(heads, axis=1) * za_ref[...].astype(F32)
    y_a = _mm(on, wdn_ref[...])
    y = g0_ref[...].astype(F32) * y_a + g1yb_ref[...].astype(F32)
    out = x_ref[...] + _mm(y, wout_ref[...])
    if final_norm:
        out = out * lax.rsqrt(jnp.mean(out * out, axis=-1, keepdims=True) + EPS) * fnw_ref[...]
    y_ref[...] = out


def _const_spec(shape):
    nd = len(shape)
    return pl.BlockSpec(shape, lambda b, i: (0,) * nd, pipeline_mode=pl.Buffered(1))


def _compiler_params():
    return pltpu.CompilerParams(dimension_semantics=("arbitrary", "arbitrary"),
                                vmem_limit_bytes=VMEM_LIMIT_BYTES)


def _proj_call(x, nw, wconv, wrest, wabt, qkvw, cfw, cfb, lnw, lnb, wcf, gateb, acol, dtcol):
    bsz, seq, _ = x.shape
    tm = TOKEN_TILE
    n_tiles = seq // tm
    hb = tm // HALO
    n_hblocks = seq // HALO

    def tile3(width, dtype):
        return jax.ShapeDtypeStruct((bsz, seq, width), dtype), pl.BlockSpec((None, tm, width), lambda b, i: (b, i, 0))

    outs = [tile3(DN_QK, BF16), tile3(DN_QK, BF16), tile3(DN_V, BF16), tile3(DN_V, BF16),
            tile3(D_MODEL, BF16), tile3(D_MODEL, BF16), tile3(2 * N_CHAIN, F32)]
    out_shapes = [o[0] for o in outs] + [jax.ShapeDtypeStruct((bsz, seq // CHUNK, 2 * N_CHAIN, CHUNK), F32)]
    out_specs = [o[1] for o in outs] + [pl.BlockSpec((None, tm // CHUNK, 2 * N_CHAIN, CHUNK),
                                                     lambda b, i: (b, i, 0, 0))]
    in_specs = [
        pl.BlockSpec((None, tm, D_MODEL), lambda b, i: (b, i, 0)),
        pl.BlockSpec((None, HALO, D_MODEL), lambda b, i: (b, jnp.maximum(i * hb - 1, 0), 0)),
        pl.BlockSpec((None, HALO, D_MODEL), lambda b, i: (b, jnp.minimum((i + 1) * hb, n_hblocks - 1), 0)),
    ] + [_const_spec(a.shape) for a in (nw, wconv, wrest, wabt, qkvw, cfw, cfb, lnw, lnb, wcf, gateb,
                                        acol, dtcol)]
    rows_h = tm + 2 * HALO
    scratch = [
        pltpu.VMEM((rows_h, D_MODEL), BF16),
        pltpu.VMEM((N_HEADS, rows_h, HEAD_DIM), F32),
        pltpu.VMEM((N_HEADS, rows_h, HEAD_DIM), F32),
        pltpu.VMEM((N_HEADS, rows_h, HEAD_DIM), F32),
        pltpu.VMEM((rows_h, 2 * CF_WIDTH), F32),
        pltpu.VMEM((CF_WIDTH // LANES, rows_h, LANES), F32),
        pltpu.VMEM((tm, CF_WIDTH), F32),
        pltpu.VMEM((tm, CF_WIDTH), BF16),
        pltpu.VMEM((tm, CF_WIDTH), F32),
        pltpu.VMEM((tm, D_MODEL), F32),
    ]
    return pl.pallas_call(
        _proj_kernel, grid=(bsz, n_tiles), in_specs=in_specs, out_specs=out_specs, out_shape=out_shapes,
        scratch_shapes=scratch, compiler_params=_compiler_params(), name="proj_prep",
    )(x, x, x, nw, wconv, wrest, wabt, qkvw, cfw, cfb, lnw, lnb, wcf, gateb, acol, dtcol)


def _delta_call(q, k, v, gbtok, gblane):
    bsz, seq, _ = q.shape
    tm = TOKEN_TILE
    n_tiles = seq // tm
    fwd3 = lambda w: pl.BlockSpec((None, tm, w), lambda b, i: (b, i, 0))
    bwd3 = lambda w: pl.BlockSpec((None, tm, w), lambda b, i: (b, n_tiles - 1 - i, 0))
    lane_f = pl.BlockSpec((None, tm // CHUNK, 2 * N_CHAIN, CHUNK), lambda b, i: (b, i, 0, 0))
    lane_b = pl.BlockSpec((None, tm // CHUNK, 2 * N_CHAIN, CHUNK), lambda b, i: (b, n_tiles - 1 - i, 0, 0))
    in_specs = [fwd3(DN_QK), fwd3(DN_QK), fwd3(DN_V), bwd3(DN_QK), bwd3(DN_QK), bwd3(DN_V),
                fwd3(2 * N_CHAIN), bwd3(2 * N_CHAIN), lane_f, lane_b]
    out_shape = [jax.ShapeDtypeStruct((bsz, seq, DN_V), BF16)] * 2
    return pl.pallas_call(
        _delta_kernel, grid=(bsz, n_tiles), in_specs=in_specs, out_specs=[fwd3(DN_V), bwd3(DN_V)],
        out_shape=out_shape, scratch_shapes=[pltpu.VMEM((N_CHAIN, HEAD_DIM, HEAD_DIM), F32)],
        compiler_params=_compiler_params(), name="delta_scan",
    )(q, k, v, q, k, v, gbtok, gbtok, gblane, gblane)


def _out_call(o_f, o_b, za, g0, g1yb, x, dnw, wdn, wout, fnw, final_norm):
    bsz, seq, _ = x.shape
    tm = TOKEN_TILE
    tile = lambda w: pl.BlockSpec((None, tm, w), lambda b, i: (b, i, 0))
    in_specs = [tile(DN_V), tile(DN_V), tile(DN_V), tile(D_MODEL), tile(D_MODEL), tile(D_MODEL)] + \
               [_const_spec(a.shape) for a in (dnw, wdn, wout, fnw)]
    return pl.pallas_call(
        functools.partial(_out_kernel, final_norm=final_norm), grid=(bsz, seq // tm), in_specs=in_specs,
        out_specs=tile(D_MODEL), out_shape=jax.ShapeDtypeStruct(x.shape, F32),
        compiler_params=_compiler_params(), name="merge_out",
    )(o_f, o_b, za, g0, g1yb, x, dnw, wdn, wout, fnw)


def kernel(x, norm_w, w_in, qkv_conv_w, a_log, dt_bias, dn_norm_w, w_dn_out, cf_conv_w, cf_conv_b,
           cf_ln_w, cf_ln_b, w_cf_out, gate_b, w_out, final_norm_w):
    depth = w_in.shape[0]
    bsz, seq, d_model = x.shape
    assert d_model == D_MODEL and seq % TOKEN_TILE == 0 and TOKEN_TILE % CHUNK == 0

    o_za = QKV
    o_ab = o_za + DN_V
    o_cf = o_ab + 2 * N_CHAIN
    o_zb = o_cf + 2 * CF_WIDTH
    o_gate = o_zb + CF_WIDTH

    for l in range(depth):
        w = w_in[l]
        wconv = jnp.concatenate([w[:, 0:QKV], w[:, o_cf:o_cf + 2 * CF_WIDTH]], axis=1).astype(BF16)
        w_ab = w[:, o_ab:o_ab + 2 * N_CHAIN]
        wrest = jnp.concatenate([w[:, o_za:o_za + DN_V], w[:, o_zb:o_zb + CF_WIDTH],
                                 w[:, o_gate:o_gate + 2 * D_MODEL]], axis=1).astype(BF16)
        wabt = w_ab.T.astype(BF16)
        pad8 = lambda t: jnp.concatenate([t.reshape(-1), jnp.zeros((N_CHAIN,), F32)])
        a16, dt16 = pad8(a_log[l]), pad8(dt_bias[l])
        q, k, v, za, g0, g1yb, gbtok, gblane = _proj_call(
            x, norm_w[l][None, :], wconv, wrest, wabt, qkv_conv_w[l], cf_conv_w[l], cf_conv_b[l][None, :],
            cf_ln_w[l][None, :], cf_ln_b[l][None, :], w_cf_out[l].astype(BF16), gate_b[l][None, :],
            a16[:, None], dt16[:, None])
        o_f, o_b = _delta_call(q, k, v, gbtok, gblane)
        x = _out_call(o_f, o_b, za, g0, g1yb, x, dn_norm_w[l][None, :], w_dn_out[l].astype(BF16),
                      w_out[l].astype(BF16), final_norm_w[None, :], final_norm=(l == depth - 1))
    return x
```

```python
import functools

import jax
import jax.numpy as jnp
from jax import lax
from jax.experimental import pallas as pl
from jax.experimental.pallas import tpu as pltpu

EPS = 1e-6
D_MODEL = 1024
N_HEADS = 4
HEAD_DIM = 128
DN_QK = N_HEADS * HEAD_DIM
DN_V = N_HEADS * HEAD_DIM
DN_CONV = 5
CF_WIDTH = 512
CF_CONV = 31
N_DIR = 2
N_CHAIN = N_DIR * N_HEADS
QKV = 2 * DN_QK + DN_V

LANES = 128
SUBLANES = 8
MXU_N = 256
V7X_VMEM_BYTES = 64 * 1024 * 1024
VMEM_LIMIT_BYTES = V7X_VMEM_BYTES - 6 * 1024 * 1024

TOKEN_TILE = 512
SCAN_TILE = 1024
HALO = 16
CHUNK = 128
ROW_BLOCK = 64
TOKEN_ROWS = 16
PIPE_LAG = 4

BF16 = jnp.bfloat16
F32 = jnp.float32


def _sigmoid(x):
    return 1.0 / (1.0 + jnp.exp(-x))


def _silu(x):
    return x * _sigmoid(x)


def _softplus(x):
    return jnp.maximum(x, 0.0) + jnp.log1p(jnp.exp(-jnp.abs(x)))


def _mm(a, b):
    return jnp.dot(a.astype(BF16), b.astype(BF16), preferred_element_type=F32)


def _mm_nt(a, b):
    return lax.dot_general(a.astype(BF16), b.astype(BF16), (((1,), (1,)), ((), ())),
                           preferred_element_type=F32)


def _split3(x):
    hi = x.astype(BF16)
    r1 = x - hi.astype(F32)
    mid = r1.astype(BF16)
    lo = (r1 - mid.astype(F32)).astype(BF16)
    return hi, mid, lo


def _plan(mxu_items, vpu_items):
    plan = []
    n, m = len(vpu_items), max(len(mxu_items), 1)
    done = 0
    for idx, item in enumerate(mxu_items):
        plan.append(("mxu", item))
        upto = (idx + 1) * n // m
        plan.extend(("vpu", unit) for unit in vpu_items[done:upto])
        done = upto
    plan.extend(("vpu", unit) for unit in vpu_items[done:])
    return plan


def _run_plan(plan):
    unit_tokens = []
    piece_tokens = []
    last_unit = None
    gate_next_unit = False
    for kind, fn in plan:
        if kind == "mxu":
            j = len(piece_tokens)
            unit_tokens.append(last_unit)
            piece_tokens.append(fn(unit_tokens[j - PIPE_LAG] if j >= PIPE_LAG else None))
            gate_next_unit = True
        else:
            j = len(piece_tokens)
            gate = piece_tokens[j - PIPE_LAG] if gate_next_unit and j >= PIPE_LAG else None
            last_unit = fn(gate)
            gate_next_unit = False


def _proj_kernel(x_ref, xp_ref, xn_ref, nw_ref, wconv_ref, wrest_ref, wabt_ref, qkvw_ref,
                 cfw_ref, cfb_ref, lnw_ref, lnb_ref, wcf_ref, gateb_ref, acol_ref, dtcol_ref,
                 q_ref, k_ref, v_ref, za_ref, g0_ref, g1yb_ref, gbtok_ref, gblane_ref,
                 hall_ref, pq_ref, pk_ref, pv_ref, pcf_ref, u_ref, cv_ref, ub_ref, zb_ref, g1_ref):
    tm = x_ref.shape[0]
    rows_h = tm + 2 * HALO
    half_h = rows_h // 2
    half = tm // 2
    i = pl.program_id(1)
    n_tiles = pl.num_programs(1)
    nw = nw_ref[...]

    def rms(xv):
        return xv * lax.rsqrt(jnp.mean(xv * xv, axis=-1, keepdims=True) + EPS) * nw

    never = pl.program_id(0) < 0

    def wait_for(token, ref, row0, col0=0):
        if token is not None:
            if ref.dtype == BF16:
                rows, other = TOKEN_ROWS, token
            else:
                rows, other = SUBLANES, token[0:SUBLANES, :].astype(F32)
            blk = ref[row0:row0 + rows, col0:col0 + LANES]
            ref[row0:row0 + rows, col0:col0 + LANES] = jnp.where(never, other, blk)

    def token_of(val):
        return val[0:TOKEN_ROWS, 0:LANES].astype(BF16)

    def conv_proj_items(dst_ref, c0, width, per_head=False):
        items = []
        for m0 in (0, half_h):
            for n0 in range(0, width, MXU_N):
                def item(gate, m0=m0, n0=n0):
                    wait_for(gate, hall_ref, m0)
                    res = jnp.dot(hall_ref[m0:m0 + half_h, :], wconv_ref[:, c0 + n0:c0 + n0 + MXU_N],
                                  preferred_element_type=F32)
                    if per_head:
                        for cc in range(0, MXU_N, HEAD_DIM):
                            dst_ref[(n0 + cc) // HEAD_DIM, m0:m0 + half_h, :] = res[:, cc:cc + HEAD_DIM]
                    else:
                        dst_ref[m0:m0 + half_h, n0:n0 + MXU_N] = res
                    return token_of(res)
                items.append(item)
        return items

    def rest_proj_items(c0, width, store):
        items = []
        for m0 in (0, half):
            for n0 in range(0, width, MXU_N):
                def item(gate, m0=m0, n0=n0):
                    wait_for(gate, hall_ref, HALO + m0)
                    res = jnp.dot(hall_ref[HALO + m0:HALO + m0 + half, :],
                                  wrest_ref[:, c0 + n0:c0 + n0 + MXU_N], preferred_element_type=F32)
                    store(slice(m0, m0 + half), slice(n0, n0 + MXU_N), res)
                    return token_of(res)
                items.append(item)
        return items

    def store_za(rs, cs, res):
        za_ref[rs, cs] = _silu(res).astype(BF16)

    def store_zb(rs, cs, res):
        zb_ref[rs, cs] = res

    def store_g0(rs, cs, res):
        g0_ref[rs, cs] = _sigmoid(res + gateb_ref[:, cs]).astype(BF16)

    def store_g1(rs, cs, res):
        g1_ref[rs, cs] = _sigmoid(res + gateb_ref[:, D_MODEL + cs.start:D_MODEL + cs.stop])

    def branch_b_out_items(pt, n_parts):
        part = tm // n_parts
        rs = slice(pt * part, (pt + 1) * part)
        items = []
        for n0 in range(0, D_MODEL, MXU_N):
            def item(gate, n0=n0):
                y_b = jnp.dot(ub_ref[rs, :], wcf_ref[:, n0:n0 + MXU_N], preferred_element_type=F32)
                g1yb_ref[rs, n0:n0 + MXU_N] = (g1_ref[rs, n0:n0 + MXU_N] * y_b).astype(BF16)
                return token_of(y_b)
            items.append(item)
        return items

    def decay_items():
        def item(gate):
            wait_for(gate, hall_ref, HALO)
            h_main = hall_ref[HALO:HALO + tm, :]
            ab_lane = lax.dot_general(wabt_ref[...], h_main, (((1,), (1,)), ((), ())),
                                      preferred_element_type=F32)
            g_lane = -jnp.exp(acol_ref[...]) * _softplus(ab_lane + dtcol_ref[...])
            beta_lane = _sigmoid(ab_lane)
            rid = lax.broadcasted_iota(jnp.int32, (CHUNK, CHUNK), 0)
            cid = lax.broadcasted_iota(jnp.int32, (CHUNK, CHUNK), 1)
            upper_b = (rid <= cid).astype(BF16)
            lower_b = (rid >= cid).astype(BF16)
            row_c = lax.broadcasted_iota(jnp.int32, (2 * N_CHAIN, CHUNK), 0)
            nr = 2 * N_CHAIN
            for c in range(tm // CHUNK):
                rows = slice(c * CHUNK, (c + 1) * CHUNK)
                parts = jnp.concatenate(_split3(g_lane[:, rows]), axis=0)
                pre = jnp.dot(parts, upper_b, preferred_element_type=F32)
                suf = jnp.dot(parts, lower_b, preferred_element_type=F32)
                pre = pre[0:nr] + pre[nr:2 * nr] + pre[2 * nr:3 * nr]
                suf = suf[0:nr] + suf[nr:2 * nr] + suf[2 * nr:3 * nr]
                blk = jnp.where(row_c < N_HEADS, pre, jnp.where(row_c < N_CHAIN, suf, beta_lane[:, rows]))
                gblane_ref[c] = blk
                gbtok_ref[rows, :] = blk.T
            return token_of(ab_lane)
        return [item]

    def short_conv_units(src_ref, group, dst_ref):
        off5 = HALO - DN_CONV // 2
        units = []
        for rb in range(tm // ROW_BLOCK):
            for h in range(N_HEADS):
                def unit(gate, r0=rb * ROW_BLOCK, c0=h * HEAD_DIM, src=src_ref.at[h]):
                    wait_for(gate, src, r0 + HALO)
                    wc = group * DN_QK + c0
                    acc = None
                    for j in range(DN_CONV):
                        t = (src[r0 + off5 + j:r0 + off5 + j + ROW_BLOCK, :]
                             * qkvw_ref[j:j + 1, wc:wc + HEAD_DIM])
                        acc = t if acc is None else acc + t
                    s = _silu(acc)
                    if group < 2:
                        s = s * lax.rsqrt(jnp.sum(s * s, axis=-1, keepdims=True) + EPS)
                    if group == 0:
                        s = s * (HEAD_DIM ** -0.5)
                    out = s.astype(BF16)
                    dst_ref[r0:r0 + ROW_BLOCK, c0:c0 + HEAD_DIM] = out
                    return token_of(out)
                units.append(unit)
        return units

    def glu_units():
        units = []
        n_split = 4
        rows = rows_h // n_split
        for p in range(n_split):
            def unit(gate, r0=p * rows):
                wait_for(gate, pcf_ref, r0)
                val = (pcf_ref[r0:r0 + rows, 0:CF_WIDTH]
                       * _sigmoid(pcf_ref[r0:r0 + rows, CF_WIDTH:2 * CF_WIDTH]))
                for cg in range(CF_WIDTH // LANES):
                    u_ref[cg, r0:r0 + rows, :] = val[:, cg * LANES:(cg + 1) * LANES]
                return token_of(val)
            units.append(unit)
        return units

    def long_conv_units(pt, n_parts):
        off31 = HALO - CF_CONV // 2
        part = tm // n_parts
        units = []
        for rb in range(part // ROW_BLOCK):
            r0 = pt * part + rb * ROW_BLOCK
            for cg in range(CF_WIDTH // LANES):
                def unit(gate, r0=r0, c0=cg * LANES, src=u_ref.at[cg]):
                    wait_for(gate, src, r0 + HALO)
                    acc = None
                    for j in range(CF_CONV):
                        t = src[r0 + off31 + j:r0 + off31 + j + ROW_BLOCK, :] * cfw_ref[j:j + 1, c0:c0 + LANES]
                        acc = t if acc is None else acc + t
                    cv_ref[r0:r0 + ROW_BLOCK, c0:c0 + LANES] = acc
                    return token_of(acc)
                units.append(unit)

            def norm_unit(gate, r0=r0):
                wait_for(gate, cv_ref, r0)
                cv = cv_ref[r0:r0 + ROW_BLOCK, :] + cfb_ref[...]
                mu = jnp.mean(cv, axis=-1, keepdims=True)
                xc = cv - mu
                y = xc * lax.rsqrt(jnp.mean(xc * xc, axis=-1, keepdims=True) + EPS) * lnw_ref[...] + lnb_ref[...]
                out = (_silu(y) * _silu(zb_ref[r0:r0 + ROW_BLOCK, :])).astype(BF16)
                ub_ref[r0:r0 + ROW_BLOCK, :] = out
                return token_of(out)
            units.append(norm_unit)
        return units

    hall_ref[0:HALO, :] = jnp.where(i > 0, rms(xp_ref[...]), 0.0).astype(BF16)
    hall_ref[HALO:HALO + half, :] = rms(x_ref[0:half, :]).astype(BF16)

    def rms_rest(gate):
        hall_ref[HALO + half:HALO + tm, :] = rms(x_ref[half:tm, :]).astype(BF16)
        tail = jnp.where(i < n_tiles - 1, rms(xn_ref[...]), 0.0).astype(BF16)
        hall_ref[HALO + tm:, :] = tail
        return token_of(tail)

    q_items = conv_proj_items(pq_ref, 0, DN_QK, per_head=True)
    n_first = len(q_items) // 2
    g_off = DN_V + CF_WIDTH
    _run_plan(
        _plan(q_items[:n_first], [rms_rest])
        + _plan(q_items[n_first:] + conv_proj_items(pk_ref, DN_QK, DN_QK, per_head=True),
                short_conv_units(pq_ref, 0, q_ref))
        + _plan(conv_proj_items(pv_ref, 2 * DN_QK, DN_V, per_head=True), short_conv_units(pk_ref, 1, k_ref))
        + _plan(conv_proj_items(pcf_ref, QKV, 2 * CF_WIDTH), short_conv_units(pv_ref, 2, v_ref))
        + _plan(rest_proj_items(DN_V, CF_WIDTH, store_zb) + decay_items(), glu_units())
        + _plan(rest_proj_items(g_off + D_MODEL, D_MODEL, store_g1) + rest_proj_items(0, DN_V, store_za),
                long_conv_units(0, 2))
        + _plan(rest_proj_items(g_off, D_MODEL, store_g0) + branch_b_out_items(0, 2), long_conv_units(1, 2))
        + _plan(branch_b_out_items(1, 2), []))


def _delta_kernel(qf_ref, kf_ref, vf_ref, qb_ref, kb_ref, vb_ref, gtf_ref, gtb_ref, glf_ref, glb_ref,
                  of_ref, ob_ref, state_ref):
    tm = qf_ref.shape[0]
    n_chunks = tm // CHUNK
    i = pl.program_id(1)

    @pl.when(i == 0)
    def _():
        state_ref[...] = jnp.zeros_like(state_ref)

    rid = lax.broadcasted_iota(jnp.int32, (CHUNK, CHUNK), 0)
    cid = lax.broadcasted_iota(jnp.int32, (CHUNK, CHUNK), 1)
    lower = rid >= cid
    upper = rid <= cid
    eye = (rid == cid).astype(F32)
    chains = range(N_CHAIN)
    same_block = [(rid >> j) == (cid >> j) for j in range(1, CHUNK.bit_length())]
    level_masks = [same_block[0]] + [same_block[j] & jnp.logical_not(same_block[j - 1])
                                     for j in range(1, len(same_block))]

    def chunk_step(n, carry):
        rows = (pl.multiple_of(n * CHUNK, CHUNK), pl.multiple_of((n_chunks - 1 - n) * CHUNK, CHUNK))
        cidx = (n, n_chunks - 1 - n)
        gt_refs = (gtf_ref, gtb_ref)
        gl_refs = (glf_ref, glb_ref)
        q_refs = (qf_ref, qb_ref)
        k_refs = (kf_ref, kb_ref)
        v_refs = (vf_ref, vb_ref)
        o_refs = (of_ref, ob_ref)

        k, decay, strict, rhs, q_dec, k_dec, e_tot, lhs_kk = [], [], [], [], [], [], [], []
        for d in range(N_DIR):
            r0 = rows[d]
            gcum_tok = gt_refs[d][pl.ds(r0, CHUNK), :]
            gcum_lane = gl_refs[d][cidx[d]]
            if d == 0:
                mask, strict_d = lower, rid > cid
            else:
                mask, strict_d = upper, rid < cid
            for h in range(N_HEADS):
                c = d * N_HEADS + h
                hs = slice(h * HEAD_DIM, (h + 1) * HEAD_DIM)
                q_c = q_refs[d][pl.ds(r0, CHUNK), hs]
                k_c = k_refs[d][pl.ds(r0, CHUNK), hs]
                v_c = v_refs[d][pl.ds(r0, CHUNK), hs].astype(F32)
                kf_c = k_c.astype(F32)
                gcol = gcum_tok[:, c:c + 1]
                grow = gcum_lane[c:c + 1, :]
                beta = gcum_tok[:, N_CHAIN + c:N_CHAIN + c + 1]
                gtot = gcum_lane[c:c + 1, CHUNK - 1:CHUNK] if d == 0 else gcum_lane[c:c + 1, 0:1]
                k_beta = kf_c * beta
                e_col = jnp.exp(gcol)
                k.append(k_c)
                strict.append(strict_d)
                decay.append(jnp.where(mask, jnp.exp(jnp.where(mask, gcol - grow, 0.0)), 0.0))
                lhs_kk.append(jnp.concatenate([k_beta.astype(BF16), q_c], axis=0))
                rhs.append(jnp.concatenate([v_c * beta, k_beta * e_col], axis=1).astype(BF16))
                q_dec.append((q_c.astype(F32) * e_col).astype(BF16))
                k_dec.append(kf_c * jnp.exp(gtot - gcol))
                e_tot.append(jnp.exp(gtot))

        kk = [_mm_nt(lhs_kk[c], k[c]) for c in chains]
        attn = [(kk[c][CHUNK:2 * CHUNK] * decay[c]).astype(BF16) for c in chains]
        lmat = [jnp.where(strict[c], kk[c][0:CHUNK] * decay[c], 0.0) for c in chains]
        t_inv = [eye - jnp.where(level_masks[0], lmat[c], 0.0) for c in chains]
        for off_mask in level_masks[1:]:
            tb = [t_inv[c].astype(BF16) for c in chains]
            w = [_mm(jnp.where(off_mask, lmat[c], 0.0), tb[c]) for c in chains]
            t_inv = [t_inv[c] - _mm(tb[c], w[c]) for c in chains]
        sol = [_mm(t_inv[c], rhs[c]) for c in chains]
        state = [state_ref[c] for c in chains]
        r = [_mm(jnp.concatenate([sol[c][:, HEAD_DIM:2 * HEAD_DIM].astype(BF16), q_dec[c]], axis=0), state[c])
             for c in chains]
        v_new = [(sol[c][:, 0:HEAD_DIM] - r[c][0:CHUNK]).astype(BF16) for c in chains]
        o = [r[c][CHUNK:2 * CHUNK] + _mm(attn[c], v_new[c]) for c in chains]
        upd = [_mm(k_dec[c].T, v_new[c]) for c in chains]
        for c in chains:
            d, h = divmod(c, N_HEADS)
            o_refs[d][pl.ds(rows[d], CHUNK), h * HEAD_DIM:(h + 1) * HEAD_DIM] = o[c].astype(BF16)
            state_ref[c] = state[c] * e_tot[c] + upd[c]
        return carry

    lax.fori_loop(0, n_chunks, chunk_step, 0)


def _out_kernel(of_ref, ob_ref, za_ref, g0_ref, g1yb_ref, x_ref, dnw_ref, wdn_ref, wout_ref, fnw_ref,
                y_ref, *, final_norm):
    o = of_ref[...].astype(F32) + ob_ref[...].astype(F32)
    heads = []
    for h in range(N_HEADS):
        oh = o[:, h * HEAD_DIM:(h + 1) * HEAD_DIM]
        heads.append(oh * lax.rsqrt(jnp.mean(oh * oh, axis=-1, keepdims=True) + EPS) * dnw_ref[...])
    on = jnp.concatenate(heads, axis=1) * za_ref[...].astype(F32)
    y_a = _mm(on, wdn_ref[...])
    y = g0_ref[...].astype(F32) * y_a + g1yb_ref[...].astype(F32)
    out = x_ref[...] + _mm(y, wout_ref[...])
    if final_norm:
        out = out * lax.rsqrt(jnp.mean(out * out, axis=-1, keepdims=True) + EPS) * fnw_ref[...]
    y_ref[...] = out


def _const_spec(shape):
    nd = len(shape)
    return pl.BlockSpec(shape, lambda b, i: (0,) * nd, pipeline_mode=pl.Buffered(1))


def _compiler_params():
    return pltpu.CompilerParams(dimension_semantics=("arbitrary", "arbitrary"),
                                vmem_limit_bytes=VMEM_LIMIT_BYTES)


def _proj_call(x, nw, wconv, wrest, wabt, qkvw, cfw, cfb, lnw, lnb, wcf, gateb, acol, dtcol):
    bsz, seq, _ = x.shape
    tm = TOKEN_TILE
    n_tiles = seq // tm
    hb = tm // HALO
    n_hblocks = seq // HALO

    def tile3(width, dtype):
        return jax.ShapeDtypeStruct((bsz, seq, width), dtype), pl.BlockSpec((None, tm, width), lambda b, i: (b, i, 0))

    outs = [tile3(DN_QK, BF16), tile3(DN_QK, BF16), tile3(DN_V, BF16), tile3(DN_V, BF16),
            tile3(D_MODEL, BF16), tile3(D_MODEL, BF16), tile3(2 * N_CHAIN, F32)]
    out_shapes = [o[0] for o in outs] + [jax.ShapeDtypeStruct((bsz, seq // CHUNK, 2 * N_CHAIN, CHUNK), F32)]
    out_specs = [o[1] for o in outs] + [pl.BlockSpec((None, tm // CHUNK, 2 * N_CHAIN, CHUNK),
                                                     lambda b, i: (b, i, 0, 0))]
    in_specs = [
        pl.BlockSpec((None, tm, D_MODEL), lambda b, i: (b, i, 0)),
        pl.BlockSpec((None, HALO, D_MODEL), lambda b, i: (b, jnp.maximum(i * hb - 1, 0), 0)),
        pl.BlockSpec((None, HALO, D_MODEL), lambda b, i: (b, jnp.minimum((i + 1) * hb, n_hblocks - 1), 0)),
    ] + [_const_spec(a.shape) for a in (nw, wconv, wrest, wabt, qkvw, cfw, cfb, lnw, lnb, wcf, gateb,
                                        acol, dtcol)]
    rows_h = tm + 2 * HALO
    scratch = [
        pltpu.VMEM((rows_h, D_MODEL), BF16),
        pltpu.VMEM((N_HEADS, rows_h, HEAD_DIM), F32),
        pltpu.VMEM((N_HEADS, rows_h, HEAD_DIM), F32),
        pltpu.VMEM((N_HEADS, rows_h, HEAD_DIM), F32),
        pltpu.VMEM((rows_h, 2 * CF_WIDTH), F32),
        pltpu.VMEM((CF_WIDTH // LANES, rows_h, LANES), F32),
        pltpu.VMEM((tm, CF_WIDTH), F32),
        pltpu.VMEM((tm, CF_WIDTH), BF16),
        pltpu.VMEM((tm, CF_WIDTH), F32),
        pltpu.VMEM((tm, D_MODEL), F32),
    ]
    return pl.pallas_call(
        _proj_kernel, grid=(bsz, n_tiles), in_specs=in_specs, out_specs=out_specs, out_shape=out_shapes,
        scratch_shapes=scratch, compiler_params=_compiler_params(), name="proj_prep",
    )(x, x, x, nw, wconv, wrest, wabt, qkvw, cfw, cfb, lnw, lnb, wcf, gateb, acol, dtcol)


def _delta_call(q, k, v, gbtok, gblane):
    bsz, seq, _ = q.shape
    tm = SCAN_TILE
    n_tiles = seq // tm
    fwd3 = lambda w: pl.BlockSpec((None, tm, w), lambda b, i: (b, i, 0))
    bwd3 = lambda w: pl.BlockSpec((None, tm, w), lambda b, i: (b, n_tiles - 1 - i, 0))
    lane_f = pl.BlockSpec((None, tm // CHUNK, 2 * N_CHAIN, CHUNK), lambda b, i: (b, i, 0, 0))
    lane_b = pl.BlockSpec((None, tm // CHUNK, 2 * N_CHAIN, CHUNK), lambda b, i: (b, n_tiles - 1 - i, 0, 0))
    in_specs = [fwd3(DN_QK), fwd3(DN_QK), fwd3(DN_V), bwd3(DN_QK), bwd3(DN_QK), bwd3(DN_V),
                fwd3(2 * N_CHAIN), bwd3(2 * N_CHAIN), lane_f, lane_b]
    out_shape = [jax.ShapeDtypeStruct((bsz, seq, DN_V), BF16)] * 2
    return pl.pallas_call(
        _delta_kernel, grid=(bsz, n_tiles), in_specs=in_specs, out_specs=[fwd3(DN_V), bwd3(DN_V)],
        out_shape=out_shape, scratch_shapes=[pltpu.VMEM((N_CHAIN, HEAD_DIM, HEAD_DIM), F32)],
        compiler_params=_compiler_params(), name="delta_scan",
    )(q, k, v, q, k, v, gbtok, gbtok, gblane, gblane)


def _out_call(o_f, o_b, za, g0, g1yb, x, dnw, wdn, wout, fnw, final_norm):
    bsz, seq, _ = x.shape
    tm = SCAN_TILE
    tile = lambda w: pl.BlockSpec((None, tm, w), lambda b, i: (b, i, 0))
    in_specs = [tile(DN_V), tile(DN_V), tile(DN_V), tile(D_MODEL), tile(D_MODEL), tile(D_MODEL)] + \
               [_const_spec(a.shape) for a in (dnw, wdn, wout, fnw)]
    return pl.pallas_call(
        functools.partial(_out_kernel, final_norm=final_norm), grid=(bsz, seq // tm), in_specs=in_specs,
        out_specs=tile(D_MODEL), out_shape=jax.ShapeDtypeStruct(x.shape, F32),
        compiler_params=_compiler_params(), name="merge_out",
    )(o_f, o_b, za, g0, g1yb, x, dnw, wdn, wout, fnw)


def kernel(x, norm_w, w_in, qkv_conv_w, a_log, dt_bias, dn_norm_w, w_dn_out, cf_conv_w, cf_conv_b,
           cf_ln_w, cf_ln_b, w_cf_out, gate_b, w_out, final_norm_w):
    depth = w_in.shape[0]
    bsz, seq, d_model = x.shape
    assert d_model == D_MODEL and seq % SCAN_TILE == 0 and SCAN_TILE % TOKEN_TILE == 0 and TOKEN_TILE % CHUNK == 0

    o_za = QKV
    o_ab = o_za + DN_V
    o_cf = o_ab + 2 * N_CHAIN
    o_zb = o_cf + 2 * CF_WIDTH
    o_gate = o_zb + CF_WIDTH

    for l in range(depth):
        w = w_in[l]
        wconv = jnp.concatenate([w[:, 0:QKV], w[:, o_cf:o_cf + 2 * CF_WIDTH]], axis=1).astype(BF16)
        w_ab = w[:, o_ab:o_ab + 2 * N_CHAIN]
        wrest = jnp.concatenate([w[:, o_za:o_za + DN_V], w[:, o_zb:o_zb + CF_WIDTH],
                                 w[:, o_gate:o_gate + 2 * D_MODEL]], axis=1).astype(BF16)
        wabt = w_ab.T.astype(BF16)
        pad8 = lambda t: jnp.concatenate([t.reshape(-1), jnp.zeros((N_CHAIN,), F32)])
        a16, dt16 = pad8(a_log[l]), pad8(dt_bias[l])
        q, k, v, za, g0, g1yb, gbtok, gblane = _proj_call(
            x, norm_w[l][None, :], wconv, wrest, wabt, qkv_conv_w[l], cf_conv_w[l], cf_conv_b[l][None, :],
            cf_ln_w[l][None, :], cf_ln_b[l][None, :], w_cf_out[l].astype(BF16), gate_b[l][None, :],
            a16[:, None], dt16[:, None])
        o_f, o_b = _delta_call(q, k, v, gbtok, gblane)
        x = _out_call(o_f, o_b, za, g0, g1yb, x, dn_norm_w[l][None, :], w_dn_out[l].astype(BF16),
                      w_out[l].astype(BF16), final_norm_w[None, :], final_norm=(l == depth - 1))
    return x
```

```python
import functools

import jax
import jax.numpy as jnp
from jax import lax
from jax.experimental import pallas as pl
from jax.experimental.pallas import tpu as pltpu

EPS = 1e-6
D_MODEL = 1024
N_HEADS = 4
HEAD_DIM = 128
DN_QK = N_HEADS * HEAD_DIM
DN_V = N_HEADS * HEAD_DIM
DN_CONV = 5
CF_WIDTH = 512
CF_CONV = 31
N_DIR = 2
N_CHAIN = N_DIR * N_HEADS
QKV = 2 * DN_QK + DN_V

LANES = 128
SUBLANES = 8
MXU_N = 256
V7X_VMEM_BYTES = 64 * 1024 * 1024
VMEM_LIMIT_BYTES = V7X_VMEM_BYTES - 6 * 1024 * 1024

TOKEN_TILE = 512
SCAN_TILE = 1024
SCAN_ROWS = 2
HALO = 16
CHUNK = 128
ROW_BLOCK = 64
TOKEN_ROWS = 16
PIPE_LAG = 4

BF16 = jnp.bfloat16
F32 = jnp.float32


def _sigmoid(x):
    return 1.0 / (1.0 + jnp.exp(-x))


def _silu(x):
    return x * _sigmoid(x)


def _softplus(x):
    return jnp.maximum(x, 0.0) + jnp.log1p(jnp.exp(-jnp.abs(x)))


def _mm(a, b):
    return jnp.dot(a.astype(BF16), b.astype(BF16), preferred_element_type=F32)


def _mm_nt(a, b):
    return lax.dot_general(a.astype(BF16), b.astype(BF16), (((1,), (1,)), ((), ())),
                           preferred_element_type=F32)


def _split3(x):
    hi = x.astype(BF16)
    r1 = x - hi.astype(F32)
    mid = r1.astype(BF16)
    lo = (r1 - mid.astype(F32)).astype(BF16)
    return hi, mid, lo


def _plan(mxu_items, vpu_items):
    plan = []
    n, m = len(vpu_items), max(len(mxu_items), 1)
    done = 0
    for idx, item in enumerate(mxu_items):
        plan.append(("mxu", item))
        upto = (idx + 1) * n // m
        plan.extend(("vpu", unit) for unit in vpu_items[done:upto])
        done = upto
    plan.extend(("vpu", unit) for unit in vpu_items[done:])
    return plan


def _run_plan(plan):
    unit_tokens = []
    piece_tokens = []
    last_unit = None
    gate_next_unit = False
    for kind, fn in plan:
        if kind == "mxu":
            j = len(piece_tokens)
            unit_tokens.append(last_unit)
            piece_tokens.append(fn(unit_tokens[j - PIPE_LAG] if j >= PIPE_LAG else None))
            gate_next_unit = True
        else:
            j = len(piece_tokens)
            gate = piece_tokens[j - PIPE_LAG] if gate_next_unit and j >= PIPE_LAG else None
            last_unit = fn(gate)
            gate_next_unit = False


def _proj_kernel(x_ref, xp_ref, xn_ref, nw_ref, wconv_ref, wrest_ref, wabt_ref, qkvw_ref,
                 cfw_ref, cfb_ref, lnw_ref, lnb_ref, wcf_ref, gateb_ref, acol_ref, dtcol_ref,
                 q_ref, k_ref, v_ref, za_ref, g0_ref, g1yb_ref, gbtok_ref, gblane_ref,
                 hall_ref, pq_ref, pk_ref, pv_ref, pcf_ref, u_ref, cv_ref, ub_ref, zb_ref, g1_ref):
    tm = x_ref.shape[0]
    rows_h = tm + 2 * HALO
    half_h = rows_h // 2
    half = tm // 2
    i = pl.program_id(1)
    n_tiles = pl.num_programs(1)
    nw = nw_ref[...]

    def rms(xv):
        return xv * lax.rsqrt(jnp.mean(xv * xv, axis=-1, keepdims=True) + EPS) * nw

    never = pl.program_id(0) < 0

    def wait_for(token, ref, row0, col0=0):
        if token is not None:
            if ref.dtype == BF16:
                rows, other = TOKEN_ROWS, token
            else:
                rows, other = SUBLANES, token[0:SUBLANES, :].astype(F32)
            blk = ref[row0:row0 + rows, col0:col0 + LANES]
            ref[row0:row0 + rows, col0:col0 + LANES] = jnp.where(never, other, blk)

    def token_of(val):
        return val[0:TOKEN_ROWS, 0:LANES].astype(BF16)

    def conv_proj_items(dst_ref, c0, width, per_head=False):
        items = []
        for m0 in (0, half_h):
            for n0 in range(0, width, MXU_N):
                def item(gate, m0=m0, n0=n0):
                    wait_for(gate, hall_ref, m0)
                    res = jnp.dot(hall_ref[m0:m0 + half_h, :], wconv_ref[:, c0 + n0:c0 + n0 + MXU_N],
                                  preferred_element_type=F32)
                    if per_head:
                        for cc in range(0, MXU_N, HEAD_DIM):
                            dst_ref[(n0 + cc) // HEAD_DIM, m0:m0 + half_h, :] = res[:, cc:cc + HEAD_DIM]
                    else:
                        dst_ref[m0:m0 + half_h, n0:n0 + MXU_N] = res
                    return token_of(res)
                items.append(item)
        return items

    def rest_proj_items(c0, width, store):
        items = []
        for m0 in (0, half):
            for n0 in range(0, width, MXU_N):
                def item(gate, m0=m0, n0=n0):
                    wait_for(gate, hall_ref, HALO + m0)
                    res = jnp.dot(hall_ref[HALO + m0:HALO + m0 + half, :],
                                  wrest_ref[:, c0 + n0:c0 + n0 + MXU_N], preferred_element_type=F32)
                    store(slice(m0, m0 + half), slice(n0, n0 + MXU_N), res)
                    return token_of(res)
                items.append(item)
        return items

    def store_za(rs, cs, res):
        za_ref[rs, cs] = _silu(res).astype(BF16)

    def store_zb(rs, cs, res):
        zb_ref[rs, cs] = res

    def store_g0(rs, cs, res):
        g0_ref[rs, cs] = _sigmoid(res + gateb_ref[:, cs]).astype(BF16)

    def store_g1(rs, cs, res):
        g1_ref[rs, cs] = _sigmoid(res + gateb_ref[:, D_MODEL + cs.start:D_MODEL + cs.stop])

    def branch_b_out_items(pt, n_parts):
        part = tm // n_parts
        rs = slice(pt * part, (pt + 1) * part)
        items = []
        for n0 in range(0, D_MODEL, MXU_N):
            def item(gate, n0=n0):
                y_b = jnp.dot(ub_ref[rs, :], wcf_ref[:, n0:n0 + MXU_N], preferred_element_type=F32)
                g1yb_ref[rs, n0:n0 + MXU_N] = (g1_ref[rs, n0:n0 + MXU_N] * y_b).astype(BF16)
                return token_of(y_b)
            items.append(item)
        return items

    def decay_items():
        def item(gate):
            wait_for(gate, hall_ref, HALO)
            h_main = hall_ref[HALO:HALO + tm, :]
            ab_lane = lax.dot_general(wabt_ref[...], h_main, (((1,), (1,)), ((), ())),
                                      preferred_element_type=F32)
            g_lane = -jnp.exp(acol_ref[...]) * _softplus(ab_lane + dtcol_ref[...])
            beta_lane = _sigmoid(ab_lane)
            rid = lax.broadcasted_iota(jnp.int32, (CHUNK, CHUNK), 0)
            cid = lax.broadcasted_iota(jnp.int32, (CHUNK, CHUNK), 1)
            upper_b = (rid <= cid).astype(BF16)
            lower_b = (rid >= cid).astype(BF16)
            row_c = lax.broadcasted_iota(jnp.int32, (2 * N_CHAIN, CHUNK), 0)
            nr = 2 * N_CHAIN
            for c in range(tm // CHUNK):
                rows = slice(c * CHUNK, (c + 1) * CHUNK)
                parts = jnp.concatenate(_split3(g_lane[:, rows]), axis=0)
                pre = jnp.dot(parts, upper_b, preferred_element_type=F32)
                suf = jnp.dot(parts, lower_b, preferred_element_type=F32)
                pre = pre[0:nr] + pre[nr:2 * nr] + pre[2 * nr:3 * nr]
                suf = suf[0:nr] + suf[nr:2 * nr] + suf[2 * nr:3 * nr]
                blk = jnp.where(row_c < N_HEADS, pre, jnp.where(row_c < N_CHAIN, suf, beta_lane[:, rows]))
                gblane_ref[c] = blk
                gbtok_ref[rows, :] = blk.T
            return token_of(ab_lane)
        return [item]

    def short_conv_units(src_ref, group, dst_ref):
        off5 = HALO - DN_CONV // 2
        units = []
        for rb in range(tm // ROW_BLOCK):
            for h in range(N_HEADS):
                def unit(gate, r0=rb * ROW_BLOCK, c0=h * HEAD_DIM, src=src_ref.at[h]):
                    wait_for(gate, src, r0 + HALO)
                    wc = group * DN_QK + c0
                    acc = None
                    for j in range(DN_CONV):
                        t = (src[r0 + off5 + j:r0 + off5 + j + ROW_BLOCK, :]
                             * qkvw_ref[j:j + 1, wc:wc + HEAD_DIM])
                        acc = t if acc is None else acc + t
                    s = _silu(acc)
                    if group < 2:
                        s = s * lax.rsqrt(jnp.sum(s * s, axis=-1, keepdims=True) + EPS)
                    if group == 0:
                        s = s * (HEAD_DIM ** -0.5)
                    out = s.astype(BF16)
                    dst_ref[r0:r0 + ROW_BLOCK, c0:c0 + HEAD_DIM] = out
                    return token_of(out)
                units.append(unit)
        return units

    def glu_units():
        units = []
        n_split = 4
        rows = rows_h // n_split
        for p in range(n_split):
            def unit(gate, r0=p * rows):
                wait_for(gate, pcf_ref, r0)
                val = (pcf_ref[r0:r0 + rows, 0:CF_WIDTH]
                       * _sigmoid(pcf_ref[r0:r0 + rows, CF_WIDTH:2 * CF_WIDTH]))
                for cg in range(CF_WIDTH // LANES):
                    u_ref[cg, r0:r0 + rows, :] = val[:, cg * LANES:(cg + 1) * LANES]
                return token_of(val)
            units.append(unit)
        return units

    def long_conv_units(pt, n_parts):
        off31 = HALO - CF_CONV // 2
        part = tm // n_parts
        units = []
        for rb in range(part // ROW_BLOCK):
            r0 = pt * part + rb * ROW_BLOCK
            for cg in range(CF_WIDTH // LANES):
                def unit(gate, r0=r0, c0=cg * LANES, src=u_ref.at[cg]):
                    wait_for(gate, src, r0 + HALO)
                    acc = None
                    for j in range(CF_CONV):
                        t = src[r0 + off31 + j:r0 + off31 + j + ROW_BLOCK, :] * cfw_ref[j:j + 1, c0:c0 + LANES]
                        acc = t if acc is None else acc + t
                    cv_ref[r0:r0 + ROW_BLOCK, c0:c0 + LANES] = acc
                    return token_of(acc)
                units.append(unit)

            def norm_unit(gate, r0=r0):
                wait_for(gate, cv_ref, r0)
                cv = cv_ref[r0:r0 + ROW_BLOCK, :] + cfb_ref[...]
                mu = jnp.mean(cv, axis=-1, keepdims=True)
                xc = cv - mu
                y = xc * lax.rsqrt(jnp.mean(xc * xc, axis=-1, keepdims=True) + EPS) * lnw_ref[...] + lnb_ref[...]
                out = (_silu(y) * _silu(zb_ref[r0:r0 + ROW_BLOCK, :])).astype(BF16)
                ub_ref[r0:r0 + ROW_BLOCK, :] = out
                return token_of(out)
            units.append(norm_unit)
        return units

    hall_ref[0:HALO, :] = jnp.where(i > 0, rms(xp_ref[...]), 0.0).astype(BF16)
    hall_ref[HALO:HALO + half, :] = rms(x_ref[0:half, :]).astype(BF16)

    def rms_rest(gate):
        hall_ref[HALO + half:HALO + tm, :] = rms(x_ref[half:tm, :]).astype(BF16)
        tail = jnp.where(i < n_tiles - 1, rms(xn_ref[...]), 0.0).astype(BF16)
        hall_ref[HALO + tm:, :] = tail
        return token_of(tail)

    q_items = conv_proj_items(pq_ref, 0, DN_QK, per_head=True)
    n_first = len(q_items) // 2
    g_off = DN_V + CF_WIDTH
    _run_plan(
        _plan(q_items[:n_first], [rms_rest])
        + _plan(q_items[n_first:] + conv_proj_items(pk_ref, DN_QK, DN_QK, per_head=True),
                short_conv_units(pq_ref, 0, q_ref))
        + _plan(conv_proj_items(pv_ref, 2 * DN_QK, DN_V, per_head=True), short_conv_units(pk_ref, 1, k_ref))
        + _plan(conv_proj_items(pcf_ref, QKV, 2 * CF_WIDTH), short_conv_units(pv_ref, 2, v_ref))
        + _plan(rest_proj_items(DN_V, CF_WIDTH, store_zb) + decay_items(), glu_units())
        + _plan(rest_proj_items(g_off + D_MODEL, D_MODEL, store_g1) + rest_proj_items(0, DN_V, store_za),
                long_conv_units(0, 2))
        + _plan(rest_proj_items(g_off, D_MODEL, store_g0) + branch_b_out_items(0, 2), long_conv_units(1, 2))
        + _plan(branch_b_out_items(1, 2), []))


def _delta_kernel(qf_ref, kf_ref, vf_ref, qb_ref, kb_ref, vb_ref, gtf_ref, gtb_ref, glf_ref, glb_ref,
                  of_ref, ob_ref, state_ref):
    n_rows, tm = qf_ref.shape[0], qf_ref.shape[1]
    n_chunks = tm // CHUNK
    i = pl.program_id(1)

    @pl.when(i == 0)
    def _():
        state_ref[...] = jnp.zeros_like(state_ref)

    rid = lax.broadcasted_iota(jnp.int32, (CHUNK, CHUNK), 0)
    cid = lax.broadcasted_iota(jnp.int32, (CHUNK, CHUNK), 1)
    lower = rid >= cid
    upper = rid <= cid
    eye = (rid == cid).astype(F32)
    chains = range(n_rows * N_CHAIN)
    same_block = [(rid >> j) == (cid >> j) for j in range(1, CHUNK.bit_length())]
    level_masks = [same_block[0]] + [same_block[j] & jnp.logical_not(same_block[j - 1])
                                     for j in range(1, len(same_block))]

    def chunk_step(n, carry):
        rows = (pl.multiple_of(n * CHUNK, CHUNK), pl.multiple_of((n_chunks - 1 - n) * CHUNK, CHUNK))
        cidx = (n, n_chunks - 1 - n)
        gt_refs = (gtf_ref, gtb_ref)
        gl_refs = (glf_ref, glb_ref)
        q_refs = (qf_ref, qb_ref)
        k_refs = (kf_ref, kb_ref)
        v_refs = (vf_ref, vb_ref)
        o_refs = (of_ref, ob_ref)

        k, decay, strict, rhs, q_dec, k_dec, e_tot, lhs_kk = [], [], [], [], [], [], [], []
        for bb, d in [(bb, d) for bb in range(n_rows) for d in range(N_DIR)]:
            r0 = rows[d]
            gcum_tok = gt_refs[d][bb, pl.ds(r0, CHUNK), :]
            gcum_lane = gl_refs[d][bb, cidx[d]]
            if d == 0:
                mask, strict_d = lower, rid > cid
            else:
                mask, strict_d = upper, rid < cid
            for h in range(N_HEADS):
                c = d * N_HEADS + h
                hs = slice(h * HEAD_DIM, (h + 1) * HEAD_DIM)
                q_c = q_refs[d][bb, pl.ds(r0, CHUNK), hs]
                k_c = k_refs[d][bb, pl.ds(r0, CHUNK), hs]
                v_c = v_refs[d][bb, pl.ds(r0, CHUNK), hs].astype(F32)
                kf_c = k_c.astype(F32)
                gcol = gcum_tok[:, c:c + 1]
                grow = gcum_lane[c:c + 1, :]
                beta = gcum_tok[:, N_CHAIN + c:N_CHAIN + c + 1]
                gtot = gcum_lane[c:c + 1, CHUNK - 1:CHUNK] if d == 0 else gcum_lane[c:c + 1, 0:1]
                k_beta = kf_c * beta
                e_col = jnp.exp(gcol)
                k.append(k_c)
                strict.append(strict_d)
                decay.append(jnp.where(mask, jnp.exp(jnp.where(mask, gcol - grow, 0.0)), 0.0))
                lhs_kk.append(jnp.concatenate([k_beta.astype(BF16), q_c], axis=0))
                rhs.append(jnp.concatenate([v_c * beta, k_beta * e_col], axis=1).astype(BF16))
                q_dec.append((q_c.astype(F32) * e_col).astype(BF16))
                k_dec.append(kf_c * jnp.exp(gtot - gcol))
                e_tot.append(jnp.exp(gtot))

        kk = [_mm_nt(lhs_kk[c], k[c]) for c in chains]
        attn = [(kk[c][CHUNK:2 * CHUNK] * decay[c]).astype(BF16) for c in chains]
        lmat = [jnp.where(strict[c], kk[c][0:CHUNK] * decay[c], 0.0) for c in chains]
        t_inv = [eye - jnp.where(level_masks[0], lmat[c], 0.0) for c in chains]
        for off_mask in level_masks[1:]:
            tb = [t_inv[c].astype(BF16) for c in chains]
            w = [_mm(jnp.where(off_mask, lmat[c], 0.0), tb[c]) for c in chains]
            t_inv = [t_inv[c] - _mm(tb[c], w[c]) for c in chains]
        sol = [_mm(t_inv[c], rhs[c]) for c in chains]
        state = [state_ref[c] for c in chains]
        r = [_mm(jnp.concatenate([sol[c][:, HEAD_DIM:2 * HEAD_DIM].astype(BF16), q_dec[c]], axis=0), state[c])
             for c in chains]
        v_new = [(sol[c][:, 0:HEAD_DIM] - r[c][0:CHUNK]).astype(BF16) for c in chains]
        o = [r[c][CHUNK:2 * CHUNK] + _mm(attn[c], v_new[c]) for c in chains]
        upd = [_mm(k_dec[c].T, v_new[c]) for c in chains]
        for c in chains:
            bb, d, h = c // N_CHAIN, (c % N_CHAIN) // N_HEADS, c % N_HEADS
            o_refs[d][bb, pl.ds(rows[d], CHUNK), h * HEAD_DIM:(h + 1) * HEAD_DIM] = o[c].astype(BF16)
            state_ref[c] = state[c] * e_tot[c] + upd[c]
        return carry

    lax.fori_loop(0, n_chunks, chunk_step, 0)


def _out_kernel(of_ref, ob_ref, za_ref, g0_ref, g1yb_ref, x_ref, dnw_ref, wdn_ref, wout_ref, fnw_ref,
                y_ref, *, final_norm):
    o = of_ref[...].astype(F32) + ob_ref[...].astype(F32)
    heads = []
    for h in range(N_HEADS):
        oh = o[:, h * HEAD_DIM:(h + 1) * HEAD_DIM]
        heads.append(oh * lax.rsqrt(jnp.mean(oh * oh, axis=-1, keepdims=True) + EPS) * dnw_ref[...])
    on = jnp.concatenate(heads, axis=1) * za_ref[...].astype(F32)
    y_a = _mm(on, wdn_ref[...])
    y = g0_ref[...].astype(F32) * y_a + g1yb_ref[...].astype(F32)
    out = x_ref[...] + _mm(y, wout_ref[...])
    if final_norm:
        out = out * lax.rsqrt(jnp.mean(out * out, axis=-1, keepdims=True) + EPS) * fnw_ref[...]
    y_ref[...] = out


def _const_spec(shape):
    nd = len(shape)
    return pl.BlockSpec(shape, lambda b, i: (0,) * nd, pipeline_mode=pl.Buffered(1))


def _compiler_params():
    return pltpu.CompilerParams(dimension_semantics=("arbitrary", "arbitrary"),
                                vmem_limit_bytes=VMEM_LIMIT_BYTES)


def _proj_call(x, nw, wconv, wrest, wabt, qkvw, cfw, cfb, lnw, lnb, wcf, gateb, acol, dtcol):
    bsz, seq, _ = x.shape
    tm = TOKEN_TILE
    n_tiles = seq // tm
    hb = tm // HALO
    n_hblocks = seq // HALO

    def tile3(width, dtype):
        return jax.ShapeDtypeStruct((bsz, seq, width), dtype), pl.BlockSpec((None, tm, width), lambda b, i: (b, i, 0))

    outs = [tile3(DN_QK, BF16), tile3(DN_QK, BF16), tile3(DN_V, BF16), tile3(DN_V, BF16),
            tile3(D_MODEL, BF16), tile3(D_MODEL, BF16), tile3(2 * N_CHAIN, F32)]
    out_shapes = [o[0] for o in outs] + [jax.ShapeDtypeStruct((bsz, seq // CHUNK, 2 * N_CHAIN, CHUNK), F32)]
    out_specs = [o[1] for o in outs] + [pl.BlockSpec((None, tm // CHUNK, 2 * N_CHAIN, CHUNK),
                                                     lambda b, i: (b, i, 0, 0))]
    in_specs = [
        pl.BlockSpec((None, tm, D_MODEL), lambda b, i: (b, i, 0)),
        pl.BlockSpec((None, HALO, D_MODEL), lambda b, i: (b, jnp.maximum(i * hb - 1, 0), 0)),
        pl.BlockSpec((None, HALO, D_MODEL), lambda b, i: (b, jnp.minimum((i + 1) * hb, n_hblocks - 1), 0)),
    ] + [_const_spec(a.shape) for a in (nw, wconv, wrest, wabt, qkvw, cfw, cfb, lnw, lnb, wcf, gateb,
                                        acol, dtcol)]
    rows_h = tm + 2 * HALO
    scratch = [
        pltpu.VMEM((rows_h, D_MODEL), BF16),
        pltpu.VMEM((N_HEADS, rows_h, HEAD_DIM), F32),
        pltpu.VMEM((N_HEADS, rows_h, HEAD_DIM), F32),
        pltpu.VMEM((N_HEADS, rows_h, HEAD_DIM), F32),
        pltpu.VMEM((rows_h, 2 * CF_WIDTH), F32),
        pltpu.VMEM((CF_WIDTH // LANES, rows_h, LANES), F32),
        pltpu.VMEM((tm, CF_WIDTH), F32),
        pltpu.VMEM((tm, CF_WIDTH), BF16),
        pltpu.VMEM((tm, CF_WIDTH), F32),
        pltpu.VMEM((tm, D_MODEL), F32),
    ]
    return pl.pallas_call(
        _proj_kernel, grid=(bsz, n_tiles), in_specs=in_specs, out_specs=out_specs, out_shape=out_shapes,
        scratch_shapes=scratch, compiler_params=_compiler_params(), name="proj_prep",
    )(x, x, x, nw, wconv, wrest, wabt, qkvw, cfw, cfb, lnw, lnb, wcf, gateb, acol, dtcol)


def _delta_call(q, k, v, gbtok, gblane):
    bsz, seq, _ = q.shape
    tm = SCAN_TILE
    n_tiles = seq // tm
    nr = SCAN_ROWS
    fwd3 = lambda w: pl.BlockSpec((nr, tm, w), lambda b, i: (b, i, 0))
    bwd3 = lambda w: pl.BlockSpec((nr, tm, w), lambda b, i: (b, n_tiles - 1 - i, 0))
    lane_f = pl.BlockSpec((nr, tm // CHUNK, 2 * N_CHAIN, CHUNK), lambda b, i: (b, i, 0, 0))
    lane_b = pl.BlockSpec((nr, tm // CHUNK, 2 * N_CHAIN, CHUNK), lambda b, i: (b, n_tiles - 1 - i, 0, 0))
    in_specs = [fwd3(DN_QK), fwd3(DN_QK), fwd3(DN_V), bwd3(DN_QK), bwd3(DN_QK), bwd3(DN_V),
                fwd3(2 * N_CHAIN), bwd3(2 * N_CHAIN), lane_f, lane_b]
    out_shape = [jax.ShapeDtypeStruct((bsz, seq, DN_V), BF16)] * 2
    return pl.pallas_call(
        _delta_kernel, grid=(bsz // nr, n_tiles), in_specs=in_specs, out_specs=[fwd3(DN_V), bwd3(DN_V)],
        out_shape=out_shape, scratch_shapes=[pltpu.VMEM((nr * N_CHAIN, HEAD_DIM, HEAD_DIM), F32)],
        compiler_params=_compiler_params(), name="delta_scan",
    )(q, k, v, q, k, v, gbtok, gbtok, gblane, gblane)


def _out_call(o_f, o_b, za, g0, g1yb, x, dnw, wdn, wout, fnw, final_norm):
    bsz, seq, _ = x.shape
    tm = SCAN_TILE
    tile = lambda w: pl.BlockSpec((None, tm, w), lambda b, i: (b, i, 0))
    in_specs = [tile(DN_V), tile(DN_V), tile(DN_V), tile(D_MODEL), tile(D_MODEL), tile(D_MODEL)] + \
               [_const_spec(a.shape) for a in (dnw, wdn, wout, fnw)]
    return pl.pallas_call(
        functools.partial(_out_kernel, final_norm=final_norm), grid=(bsz, seq // tm), in_specs=in_specs,
        out_specs=tile(D_MODEL), out_shape=jax.ShapeDtypeStruct(x.shape, F32),
        compiler_params=_compiler_params(), name="merge_out",
    )(o_f, o_b, za, g0, g1yb, x, dnw, wdn, wout, fnw)


def kernel(x, norm_w, w_in, qkv_conv_w, a_log, dt_bias, dn_norm_w, w_dn_out, cf_conv_w, cf_conv_b,
           cf_ln_w, cf_ln_b, w_cf_out, gate_b, w_out, final_norm_w):
    depth = w_in.shape[0]
    bsz, seq, d_model = x.shape
    assert d_model == D_MODEL and bsz % SCAN_ROWS == 0 and seq % SCAN_TILE == 0 and SCAN_TILE % TOKEN_TILE == 0 and TOKEN_TILE % CHUNK == 0

    o_za = QKV
    o_ab = o_za + DN_V
    o_cf = o_ab + 2 * N_CHAIN
    o_zb = o_cf + 2 * CF_WIDTH
    o_gate = o_zb + CF_WIDTH

    for l in range(depth):
        w = w_in[l]
        wconv = jnp.concatenate([w[:, 0:QKV], w[:, o_cf:o_cf + 2 * CF_WIDTH]], axis=1).astype(BF16)
        w_ab = w[:, o_ab:o_ab + 2 * N_CHAIN]
        wrest = jnp.concatenate([w[:, o_za:o_za + DN_V], w[:, o_zb:o_zb + CF_WIDTH],
                                 w[:, o_gate:o_gate + 2 * D_MODEL]], axis=1).astype(BF16)
        wabt = w_ab.T.astype(BF16)
        pad8 = lambda t: jnp.concatenate([t.reshape(-1), jnp.zeros((N_CHAIN,), F32)])
        a16, dt16 = pad8(a_log[l]), pad8(dt_bias[l])
        q, k, v, za, g0, g1yb, gbtok, gblane = _proj_call(
            x, norm_w[l][None, :], wconv, wrest, wabt, qkv_conv_w[l], cf_conv_w[l], cf_conv_b[l][None, :],
            cf_ln_w[l][None, :], cf_ln_b[l][None, :], w_cf_out[l].astype(BF16), gate_b[l][None, :],
            a16[:, None], dt16[:, None])
        o_f, o_b = _delta_call(q, k, v, gbtok, gblane)
        x = _out_call(o_f, o_b, za, g0, g1yb, x, dn_norm_w[l][None, :], w_dn_out[l].astype(BF16),
                      w_out[l].astype(BF16), final_norm_w[None, :], final_norm=(l == depth - 1))
    return x
```

```python
import functools

import jax
import jax.numpy as jnp
from jax import lax
from jax.experimental import pallas as pl
from jax.experimental.pallas import tpu as pltpu

EPS = 1e-6
D_MODEL = 1024
N_HEADS = 4
HEAD_DIM = 128
DN_QK = N_HEADS * HEAD_DIM
DN_V = N_HEADS * HEAD_DIM
DN_CONV = 5
CF_WIDTH = 512
CF_CONV = 31
N_DIR = 2
N_CHAIN = N_DIR * N_HEADS
QKV = 2 * DN_QK + DN_V

LANES = 128
SUBLANES = 8
MXU_N = 256
V7X_VMEM_BYTES = 64 * 1024 * 1024
VMEM_LIMIT_BYTES = V7X_VMEM_BYTES - 6 * 1024 * 1024

TOKEN_TILE = 512
MERGE_TILE = 1024
SCAN_TILE = 1024
SCAN_ROWS = 2
HALO = 16
CHUNK = 128
ROW_BLOCK = 64
TOKEN_ROWS = 16
PIPE_LAG = 4

BF16 = jnp.bfloat16
F32 = jnp.float32


def _sigmoid(x):
    return 1.0 / (1.0 + jnp.exp(-x))


def _silu(x):
    return x * _sigmoid(x)


def _softplus(x):
    return jnp.maximum(x, 0.0) + jnp.log1p(jnp.exp(-jnp.abs(x)))


def _mm(a, b):
    return jnp.dot(a.astype(BF16), b.astype(BF16), preferred_element_type=F32)


def _mm_nt(a, b):
    return lax.dot_general(a.astype(BF16), b.astype(BF16), (((1,), (1,)), ((), ())),
                           preferred_element_type=F32)


def _split3(x):
    hi = x.astype(BF16)
    r1 = x - hi.astype(F32)
    mid = r1.astype(BF16)
    lo = (r1 - mid.astype(F32)).astype(BF16)
    return hi, mid, lo


def _plan(mxu_items, vpu_items):
    plan = []
    n, m = len(vpu_items), max(len(mxu_items), 1)
    done = 0
    for idx, item in enumerate(mxu_items):
        plan.append(("mxu", item))
        upto = (idx + 1) * n // m
        plan.extend(("vpu", unit) for unit in vpu_items[done:upto])
        done = upto
    plan.extend(("vpu", unit) for unit in vpu_items[done:])
    return plan


def _run_plan(plan):
    unit_tokens = []
    piece_tokens = []
    last_unit = None
    gate_next_unit = False
    for kind, fn in plan:
        if kind == "mxu":
            j = len(piece_tokens)
            unit_tokens.append(last_unit)
            piece_tokens.append(fn(unit_tokens[j - PIPE_LAG] if j >= PIPE_LAG else None))
            gate_next_unit = True
        else:
            j = len(piece_tokens)
            gate = piece_tokens[j - PIPE_LAG] if gate_next_unit and j >= PIPE_LAG else None
            last_unit = fn(gate)
            gate_next_unit = False


def _proj_kernel(x_ref, xp_ref, xn_ref, nw_ref, wconv_ref, wrest_ref, wabt_ref, qkvw_ref,
                 cfw_ref, cfb_ref, lnw_ref, lnb_ref, wcf_ref, gateb_ref, acol_ref, dtcol_ref,
                 q_ref, k_ref, v_ref, za_ref, g0_ref, g1yb_ref, gbtok_ref, gblane_ref,
                 hall_ref, pq_ref, pk_ref, pv_ref, pcf_ref, u_ref, cv_ref, ub_ref, zb_ref, g1_ref):
    tm = x_ref.shape[0]
    rows_h = tm + 2 * HALO
    half_h = rows_h // 2
    half = tm // 2
    i = pl.program_id(1)
    n_tiles = pl.num_programs(1)
    nw = nw_ref[...]

    def rms(xv):
        return xv * lax.rsqrt(jnp.mean(xv * xv, axis=-1, keepdims=True) + EPS) * nw

    never = pl.program_id(0) < 0

    def wait_for(token, ref, row0, col0=0):
        if token is not None:
            if ref.dtype == BF16:
                rows, other = TOKEN_ROWS, token
            else:
                rows, other = SUBLANES, token[0:SUBLANES, :].astype(F32)
            blk = ref[row0:row0 + rows, col0:col0 + LANES]
            ref[row0:row0 + rows, col0:col0 + LANES] = jnp.where(never, other, blk)

    def token_of(val):
        return val[0:TOKEN_ROWS, 0:LANES].astype(BF16)

    def conv_proj_items(dst_ref, c0, width, per_head=False):
        items = []
        for m0 in (0, half_h):
            for n0 in range(0, width, MXU_N):
                def item(gate, m0=m0, n0=n0):
                    wait_for(gate, hall_ref, m0)
                    res = jnp.dot(hall_ref[m0:m0 + half_h, :], wconv_ref[:, c0 + n0:c0 + n0 + MXU_N],
                                  preferred_element_type=F32)
                    if per_head:
                        for cc in range(0, MXU_N, HEAD_DIM):
                            dst_ref[(n0 + cc) // HEAD_DIM, m0:m0 + half_h, :] = res[:, cc:cc + HEAD_DIM]
                    else:
                        dst_ref[m0:m0 + half_h, n0:n0 + MXU_N] = res
                    return token_of(res)
                items.append(item)
        return items

    def rest_proj_items(c0, width, store):
        items = []
        for m0 in (0, half):
            for n0 in range(0, width, MXU_N):
                def item(gate, m0=m0, n0=n0):
                    wait_for(gate, hall_ref, HALO + m0)
                    res = jnp.dot(hall_ref[HALO + m0:HALO + m0 + half, :],
                                  wrest_ref[:, c0 + n0:c0 + n0 + MXU_N], preferred_element_type=F32)
                    store(slice(m0, m0 + half), slice(n0, n0 + MXU_N), res)
                    return token_of(res)
                items.append(item)
        return items

    def store_za(rs, cs, res):
        za_ref[rs, cs] = _silu(res).astype(BF16)

    def store_zb(rs, cs, res):
        zb_ref[rs, cs] = res

    def store_g0(rs, cs, res):
        g0_ref[rs, cs] = _sigmoid(res + gateb_ref[:, cs]).astype(BF16)

    def store_g1(rs, cs, res):
        g1_ref[rs, cs] = _sigmoid(res + gateb_ref[:, D_MODEL + cs.start:D_MODEL + cs.stop])

    def branch_b_out_items(pt, n_parts):
        part = tm // n_parts
        rs = slice(pt * part, (pt + 1) * part)
        items = []
        for n0 in range(0, D_MODEL, MXU_N):
            def item(gate, n0=n0):
                y_b = jnp.dot(ub_ref[rs, :], wcf_ref[:, n0:n0 + MXU_N], preferred_element_type=F32)
                g1yb_ref[rs, n0:n0 + MXU_N] = (g1_ref[rs, n0:n0 + MXU_N] * y_b).astype(BF16)
                return token_of(y_b)
            items.append(item)
        return items

    def decay_items():
        def item(gate):
            wait_for(gate, hall_ref, HALO)
            h_main = hall_ref[HALO:HALO + tm, :]
            ab_lane = lax.dot_general(wabt_ref[...], h_main, (((1,), (1,)), ((), ())),
                                      preferred_element_type=F32)
            g_lane = -jnp.exp(acol_ref[...]) * _softplus(ab_lane + dtcol_ref[...])
            beta_lane = _sigmoid(ab_lane)
            rid = lax.broadcasted_iota(jnp.int32, (CHUNK, CHUNK), 0)
            cid = lax.broadcasted_iota(jnp.int32, (CHUNK, CHUNK), 1)
            upper_b = (rid <= cid).astype(BF16)
            lower_b = (rid >= cid).astype(BF16)
            row_c = lax.broadcasted_iota(jnp.int32, (2 * N_CHAIN, CHUNK), 0)
            nr = 2 * N_CHAIN
            for c in range(tm // CHUNK):
                rows = slice(c * CHUNK, (c + 1) * CHUNK)
                parts = jnp.concatenate(_split3(g_lane[:, rows]), axis=0)
                pre = jnp.dot(parts, upper_b, preferred_element_type=F32)
                suf = jnp.dot(parts, lower_b, preferred_element_type=F32)
                pre = pre[0:nr] + pre[nr:2 * nr] + pre[2 * nr:3 * nr]
                suf = suf[0:nr] + suf[nr:2 * nr] + suf[2 * nr:3 * nr]
                blk = jnp.where(row_c < N_HEADS, pre, jnp.where(row_c < N_CHAIN, suf, beta_lane[:, rows]))
                gblane_ref[c] = blk
                gbtok_ref[rows, :] = blk.T
            return token_of(ab_lane)
        return [item]

    def short_conv_units(src_ref, group, dst_ref):
        off5 = HALO - DN_CONV // 2
        units = []
        for rb in range(tm // ROW_BLOCK):
            for h in range(N_HEADS):
                def unit(gate, r0=rb * ROW_BLOCK, c0=h * HEAD_DIM, src=src_ref.at[h]):
                    wait_for(gate, src, r0 + HALO)
                    wc = group * DN_QK + c0
                    acc = None
                    for j in range(DN_CONV):
                        t = (src[r0 + off5 + j:r0 + off5 + j + ROW_BLOCK, :]
                             * qkvw_ref[j:j + 1, wc:wc + HEAD_DIM])
                        acc = t if acc is None else acc + t
                    s = _silu(acc)
                    if group < 2:
                        s = s * lax.rsqrt(jnp.sum(s * s, axis=-1, keepdims=True) + EPS)
                    if group == 0:
                        s = s * (HEAD_DIM ** -0.5)
                    out = s.astype(BF16)
                    dst_ref[r0:r0 + ROW_BLOCK, c0:c0 + HEAD_DIM] = out
                    return token_of(out)
                units.append(unit)
        return units

    def glu_units():
        units = []
        n_split = 4
        rows = rows_h // n_split
        for p in range(n_split):
            def unit(gate, r0=p * rows):
                wait_for(gate, pcf_ref, r0)
                val = (pcf_ref[r0:r0 + rows, 0:CF_WIDTH]
                       * _sigmoid(pcf_ref[r0:r0 + rows, CF_WIDTH:2 * CF_WIDTH]))
                for cg in range(CF_WIDTH // LANES):
                    u_ref[cg, r0:r0 + rows, :] = val[:, cg * LANES:(cg + 1) * LANES]
                return token_of(val)
            units.append(unit)
        return units

    def long_conv_units(pt, n_parts):
        off31 = HALO - CF_CONV // 2
        part = tm // n_parts
        units = []
        for rb in range(part // ROW_BLOCK):
            r0 = pt * part + rb * ROW_BLOCK
            for cg in range(CF_WIDTH // LANES):
                def unit(gate, r0=r0, c0=cg * LANES, src=u_ref.at[cg]):
                    wait_for(gate, src, r0 + HALO)
                    acc = None
                    for j in range(CF_CONV):
                        t = src[r0 + off31 + j:r0 + off31 + j + ROW_BLOCK, :] * cfw_ref[j:j + 1, c0:c0 + LANES]
                        acc = t if acc is None else acc + t
                    cv_ref[r0:r0 + ROW_BLOCK, c0:c0 + LANES] = acc
                    return token_of(acc)
                units.append(unit)

            def norm_unit(gate, r0=r0):
                wait_for(gate, cv_ref, r0)
                cv = cv_ref[r0:r0 + ROW_BLOCK, :] + cfb_ref[...]
                mu = jnp.mean(cv, axis=-1, keepdims=True)
                xc = cv - mu
                y = xc * lax.rsqrt(jnp.mean(xc * xc, axis=-1, keepdims=True) + EPS) * lnw_ref[...] + lnb_ref[...]
                out = (_silu(y) * _silu(zb_ref[r0:r0 + ROW_BLOCK, :])).astype(BF16)
                ub_ref[r0:r0 + ROW_BLOCK, :] = out
                return token_of(out)
            units.append(norm_unit)
        return units

    hall_ref[0:HALO, :] = jnp.where(i > 0, rms(xp_ref[...]), 0.0).astype(BF16)
    hall_ref[HALO:HALO + half, :] = rms(x_ref[0:half, :]).astype(BF16)

    def rms_rest(gate):
        hall_ref[HALO + half:HALO + tm, :] = rms(x_ref[half:tm, :]).astype(BF16)
        tail = jnp.where(i < n_tiles - 1, rms(xn_ref[...]), 0.0).astype(BF16)
        hall_ref[HALO + tm:, :] = tail
        return token_of(tail)

    q_items = conv_proj_items(pq_ref, 0, DN_QK, per_head=True)
    n_first = len(q_items) // 2
    g_off = DN_V + CF_WIDTH
    _run_plan(
        _plan(q_items[:n_first], [rms_rest])
        + _plan(q_items[n_first:] + conv_proj_items(pk_ref, DN_QK, DN_QK, per_head=True),
                short_conv_units(pq_ref, 0, q_ref))
        + _plan(conv_proj_items(pv_ref, 2 * DN_QK, DN_V, per_head=True), short_conv_units(pk_ref, 1, k_ref))
        + _plan(conv_proj_items(pcf_ref, QKV, 2 * CF_WIDTH), short_conv_units(pv_ref, 2, v_ref))
        + _plan(rest_proj_items(DN_V, CF_WIDTH, store_zb) + decay_items(), glu_units())
        + _plan(rest_proj_items(g_off + D_MODEL, D_MODEL, store_g1) + rest_proj_items(0, DN_V, store_za),
                long_conv_units(0, 2))
        + _plan(rest_proj_items(g_off, D_MODEL, store_g0) + branch_b_out_items(0, 2), long_conv_units(1, 2))
        + _plan(branch_b_out_items(1, 2), []))


def _delta_kernel(qf_ref, kf_ref, vf_ref, qb_ref, kb_ref, vb_ref, gtf_ref, gtb_ref, glf_ref, glb_ref,
                  of_ref, ob_ref, state_ref):
    n_rows, tm = qf_ref.shape[0], qf_ref.shape[1]
    n_chunks = tm // CHUNK
    i = pl.program_id(1)

    @pl.when(i == 0)
    def _():
        state_ref[...] = jnp.zeros_like(state_ref)

    rid = lax.broadcasted_iota(jnp.int32, (CHUNK, CHUNK), 0)
    cid = lax.broadcasted_iota(jnp.int32, (CHUNK, CHUNK), 1)
    lower = rid >= cid
    upper = rid <= cid
    eye = (rid == cid).astype(F32)
    chains = range(n_rows * N_CHAIN)
    same_block = [(rid >> j) == (cid >> j) for j in range(1, CHUNK.bit_length())]
    level_masks = [same_block[0]] + [same_block[j] & jnp.logical_not(same_block[j - 1])
                                     for j in range(1, len(same_block))]

    def chunk_step(n, carry):
        rows = (pl.multiple_of(n * CHUNK, CHUNK), pl.multiple_of((n_chunks - 1 - n) * CHUNK, CHUNK))
        cidx = (n, n_chunks - 1 - n)
        gt_refs = (gtf_ref, gtb_ref)
        gl_refs = (glf_ref, glb_ref)
        q_refs = (qf_ref, qb_ref)
        k_refs = (kf_ref, kb_ref)
        v_refs = (vf_ref, vb_ref)
        o_refs = (of_ref, ob_ref)

        k, decay, strict, rhs, q_dec, k_dec, e_tot, lhs_kk = [], [], [], [], [], [], [], []
        for bb, d in [(bb, d) for bb in range(n_rows) for d in range(N_DIR)]:
            r0 = rows[d]
            gcum_tok = gt_refs[d][bb, pl.ds(r0, CHUNK), :]
            gcum_lane = gl_refs[d][bb, cidx[d]]
            if d == 0:
                mask, strict_d = lower, rid > cid
            else:
                mask, strict_d = upper, rid < cid
            for h in range(N_HEADS):
                c = d * N_HEADS + h
                hs = slice(h * HEAD_DIM, (h + 1) * HEAD_DIM)
                q_c = q_refs[d][bb, pl.ds(r0, CHUNK), hs]
                k_c = k_refs[d][bb, pl.ds(r0, CHUNK), hs]
                v_c = v_refs[d][bb, pl.ds(r0, CHUNK), hs].astype(F32)
                kf_c = k_c.astype(F32)
                gcol = gcum_tok[:, c:c + 1]
                grow = gcum_lane[c:c + 1, :]
                beta = gcum_tok[:, N_CHAIN + c:N_CHAIN + c + 1]
                gtot = gcum_lane[c:c + 1, CHUNK - 1:CHUNK] if d == 0 else gcum_lane[c:c + 1, 0:1]
                k_beta = kf_c * beta
                e_col = jnp.exp(gcol)
                k.append(k_c)
                strict.append(strict_d)
                decay.append(jnp.where(mask, jnp.exp(jnp.where(mask, gcol - grow, 0.0)), 0.0))
                lhs_kk.append(jnp.concatenate([k_beta.astype(BF16), q_c], axis=0))
                rhs.append(jnp.concatenate([v_c * beta, k_beta * e_col], axis=1).astype(BF16))
                q_dec.append((q_c.astype(F32) * e_col).astype(BF16))
                k_dec.append(kf_c * jnp.exp(gtot - gcol))
                e_tot.append(jnp.exp(gtot))

        kk = [_mm_nt(lhs_kk[c], k[c]) for c in chains]
        attn = [(kk[c][CHUNK:2 * CHUNK] * decay[c]).astype(BF16) for c in chains]
        lmat = [jnp.where(strict[c], kk[c][0:CHUNK] * decay[c], 0.0) for c in chains]
        t_inv = [eye - jnp.where(level_masks[0], lmat[c], 0.0) for c in chains]
        for off_mask in level_masks[1:]:
            tb = [t_inv[c].astype(BF16) for c in chains]
            w = [_mm(jnp.where(off_mask, lmat[c], 0.0), tb[c]) for c in chains]
            t_inv = [t_inv[c] - _mm(tb[c], w[c]) for c in chains]
        sol = [_mm(t_inv[c], rhs[c]) for c in chains]
        state = [state_ref[c] for c in chains]
        r = [_mm(jnp.concatenate([sol[c][:, HEAD_DIM:2 * HEAD_DIM].astype(BF16), q_dec[c]], axis=0), state[c])
             for c in chains]
        v_new = [(sol[c][:, 0:HEAD_DIM] - r[c][0:CHUNK]).astype(BF16) for c in chains]
        o = [r[c][CHUNK:2 * CHUNK] + _mm(attn[c], v_new[c]) for c in chains]
        upd = [_mm(k_dec[c].T, v_new[c]) for c in chains]
        for c in chains:
            bb, d, h = c // N_CHAIN, (c % N_CHAIN) // N_HEADS, c % N_HEADS
            o_refs[d][bb, pl.ds(rows[d], CHUNK), h * HEAD_DIM:(h + 1) * HEAD_DIM] = o[c].astype(BF16)
            state_ref[c] = state[c] * e_tot[c] + upd[c]
        return carry

    lax.fori_loop(0, n_chunks, chunk_step, 0)


def _out_kernel(of_ref, ob_ref, za_ref, g0_ref, g1yb_ref, x_ref, dnw_ref, wdn_ref, wout_ref, fnw_ref,
                y_ref, *, final_norm):
    o = of_ref[...].astype(F32) + ob_ref[...].astype(F32)
    heads = []
    for h in range(N_HEADS):
        oh = o[:, h * HEAD_DIM:(h + 1) * HEAD_DIM]
        heads.append(oh * lax.rsqrt(jnp.mean(oh * oh, axis=-1, keepdims=True) + EPS) * dnw_ref[...])
    on = jnp.concatenate(heads, axis=1) * za_ref[...].astype(F32)
    y_a = _mm(on, wdn_ref[...])
    y = g0_ref[...].astype(F32) * y_a + g1yb_ref[...].astype(F32)
    out = x_ref[...] + _mm(y, wout_ref[...])
    if final_norm:
        out = out * lax.rsqrt(jnp.mean(out * out, axis=-1, keepdims=True) + EPS) * fnw_ref[...]
    y_ref[...] = out


def _layer_spec(arr, layer):
    return pl.BlockSpec((None,) + arr.shape[1:], lambda b, i: (layer, 0, 0), pipeline_mode=pl.Buffered(1))


def _compiler_params():
    return pltpu.CompilerParams(dimension_semantics=("arbitrary", "arbitrary"),
                                vmem_limit_bytes=VMEM_LIMIT_BYTES)


def _proj_call(x, layer, params):
    bsz, seq, _ = x.shape
    tm = TOKEN_TILE
    n_tiles = seq // tm
    hb = tm // HALO
    n_hblocks = seq // HALO

    def tile3(width, dtype):
        return jax.ShapeDtypeStruct((bsz, seq, width), dtype), pl.BlockSpec((None, tm, width), lambda b, i: (b, i, 0))

    outs = [tile3(DN_QK, BF16), tile3(DN_QK, BF16), tile3(DN_V, BF16), tile3(DN_V, BF16),
            tile3(D_MODEL, BF16), tile3(D_MODEL, BF16), tile3(2 * N_CHAIN, F32)]
    out_shapes = [o[0] for o in outs] + [jax.ShapeDtypeStruct((bsz, seq // CHUNK, 2 * N_CHAIN, CHUNK), F32)]
    out_specs = [o[1] for o in outs] + [pl.BlockSpec((None, tm // CHUNK, 2 * N_CHAIN, CHUNK),
                                                     lambda b, i: (b, i, 0, 0))]
    in_specs = [
        pl.BlockSpec((None, tm, D_MODEL), lambda b, i: (b, i, 0)),
        pl.BlockSpec((None, HALO, D_MODEL), lambda b, i: (b, jnp.maximum(i * hb - 1, 0), 0)),
        pl.BlockSpec((None, HALO, D_MODEL), lambda b, i: (b, jnp.minimum((i + 1) * hb, n_hblocks - 1), 0)),
    ] + [_layer_spec(a, layer) for a in params]
    rows_h = tm + 2 * HALO
    scratch = [
        pltpu.VMEM((rows_h, D_MODEL), BF16),
        pltpu.VMEM((N_HEADS, rows_h, HEAD_DIM), F32),
        pltpu.VMEM((N_HEADS, rows_h, HEAD_DIM), F32),
        pltpu.VMEM((N_HEADS, rows_h, HEAD_DIM), F32),
        pltpu.VMEM((rows_h, 2 * CF_WIDTH), F32),
        pltpu.VMEM((CF_WIDTH // LANES, rows_h, LANES), F32),
        pltpu.VMEM((tm, CF_WIDTH), F32),
        pltpu.VMEM((tm, CF_WIDTH), BF16),
        pltpu.VMEM((tm, CF_WIDTH), F32),
        pltpu.VMEM((tm, D_MODEL), F32),
    ]
    return pl.pallas_call(
        _proj_kernel, grid=(bsz, n_tiles), in_specs=in_specs, out_specs=out_specs, out_shape=out_shapes,
        scratch_shapes=scratch, compiler_params=_compiler_params(), name="proj_prep",
    )(x, x, x, *params)


def _delta_call(q, k, v, gbtok, gblane):
    bsz, seq, _ = q.shape
    tm = SCAN_TILE
    n_tiles = seq // tm
    nr = SCAN_ROWS
    fwd3 = lambda w: pl.BlockSpec((nr, tm, w), lambda b, i: (b, i, 0))
    bwd3 = lambda w: pl.BlockSpec((nr, tm, w), lambda b, i: (b, n_tiles - 1 - i, 0))
    lane_f = pl.BlockSpec((nr, tm // CHUNK, 2 * N_CHAIN, CHUNK), lambda b, i: (b, i, 0, 0))
    lane_b = pl.BlockSpec((nr, tm // CHUNK, 2 * N_CHAIN, CHUNK), lambda b, i: (b, n_tiles - 1 - i, 0, 0))
    in_specs = [fwd3(DN_QK), fwd3(DN_QK), fwd3(DN_V), bwd3(DN_QK), bwd3(DN_QK), bwd3(DN_V),
                fwd3(2 * N_CHAIN), bwd3(2 * N_CHAIN), lane_f, lane_b]
    out_shape = [jax.ShapeDtypeStruct((bsz, seq, DN_V), BF16)] * 2
    return pl.pallas_call(
        _delta_kernel, grid=(bsz // nr, n_tiles), in_specs=in_specs, out_specs=[fwd3(DN_V), bwd3(DN_V)],
        out_shape=out_shape, scratch_shapes=[pltpu.VMEM((nr * N_CHAIN, HEAD_DIM, HEAD_DIM), F32)],
        compiler_params=_compiler_params(), name="delta_scan",
    )(q, k, v, q, k, v, gbtok, gbtok, gblane, gblane)


def _out_call(o_f, o_b, za, g0, g1yb, x, layer, params, fnw, final_norm):
    bsz, seq, _ = x.shape
    tm = MERGE_TILE
    tile = lambda w: pl.BlockSpec((None, tm, w), lambda b, i: (b, i, 0))
    in_specs = [tile(DN_V), tile(DN_V), tile(DN_V), tile(D_MODEL), tile(D_MODEL), tile(D_MODEL)] + \
               [_layer_spec(a, layer) for a in params] + [_layer_spec(fnw, 0)]
    return pl.pallas_call(
        functools.partial(_out_kernel, final_norm=final_norm), grid=(bsz, seq // tm), in_specs=in_specs,
        out_specs=tile(D_MODEL), out_shape=jax.ShapeDtypeStruct(x.shape, F32),
        compiler_params=_compiler_params(), name="merge_out",
    )(o_f, o_b, za, g0, g1yb, x, *params, fnw)


def kernel(x, norm_w, w_in, qkv_conv_w, a_log, dt_bias, dn_norm_w, w_dn_out, cf_conv_w, cf_conv_b,
           cf_ln_w, cf_ln_b, w_cf_out, gate_b, w_out, final_norm_w):
    depth = w_in.shape[0]
    bsz, seq, d_model = x.shape
    assert d_model == D_MODEL and bsz % SCAN_ROWS == 0
    assert seq % MERGE_TILE == 0 and seq % TOKEN_TILE == 0 and seq % SCAN_TILE == 0
    assert TOKEN_TILE % CHUNK == 0 and SCAN_TILE % CHUNK == 0

    o_za = QKV
    o_ab = o_za + DN_V
    o_cf = o_ab + 2 * N_CHAIN
    o_zb = o_cf + 2 * CF_WIDTH
    o_gate = o_zb + CF_WIDTH

    row = lambda t: t.reshape(depth, 1, -1)
    pad8 = lambda t: jnp.concatenate([t.reshape(depth, N_CHAIN), jnp.zeros((depth, N_CHAIN), F32)],
                                     axis=1)[:, :, None]
    w_ab = w_in[:, :, o_ab:o_ab + 2 * N_CHAIN]
    proj_params = (
        row(norm_w),
        jnp.concatenate([w_in[:, :, 0:QKV], w_in[:, :, o_cf:o_cf + 2 * CF_WIDTH]], axis=2).astype(BF16),
        jnp.concatenate([w_in[:, :, o_za:o_za + DN_V], w_in[:, :, o_zb:o_zb + CF_WIDTH],
                         w_in[:, :, o_gate:o_gate + 2 * D_MODEL]], axis=2).astype(BF16),
        jnp.swapaxes(w_ab, 1, 2).astype(BF16),
        qkv_conv_w, cf_conv_w, row(cf_conv_b), row(cf_ln_w), row(cf_ln_b), w_cf_out.astype(BF16),
        row(gate_b), pad8(a_log), pad8(dt_bias))
    out_params = (row(dn_norm_w), w_dn_out.astype(BF16), w_out.astype(BF16))
    fnw = final_norm_w.reshape(1, 1, D_MODEL)

    for l in range(depth):
        q, k, v, za, g0, g1yb, gbtok, gblane = _proj_call(x, l, proj_params)
        o_f, o_b = _delta_call(q, k, v, gbtok, gblane)
        x = _out_call(o_f, o_b, za, g0, g1yb, x, l, out_params, fnw, final_norm=(l == depth - 1))
    return x
```
